```python
import jax
import jax.numpy as jnp
from jax import lax
import numpy as np

D_MODEL = 1024
BATCH = 8
SEQ = 2048
DEPTH = 4

HEAD_DIM = 64
ROT_DIM = HEAD_DIM // 4
ROPE_THETA = 500000.0
BLOCK_Q = 128
NORM_EPS = 1e-6
N_EVEN = (DEPTH + 1) // 2
N_ODD = DEPTH // 2

A_HEADS = 8
A_KV_HEADS = 2
IDX_HEADS = 8
IDX_DIM = 32
IDX_ROT = IDX_DIM // 4
TOPK_MAX = 256

B_HEADS = 8
DILATED_PATTERNS = ((128, 1), (512, 4), (2048, 16))

EVEN_SPLITS = (A_HEADS * HEAD_DIM, A_KV_HEADS * HEAD_DIM, A_KV_HEADS * HEAD_DIM,
               IDX_HEADS * IDX_DIM, IDX_DIM, IDX_HEADS,
               B_HEADS * HEAD_DIM, B_HEADS * HEAD_DIM, B_HEADS * HEAD_DIM)
EVEN_IN = sum(EVEN_SPLITS)
EVEN_MIX = (A_HEADS + B_HEADS) * HEAD_DIM

C_HEADS = 8
C_NOPE = 64
C_ROPE = 32
C_V = 64
Q_LORA = 256
KV_LORA = 128

D_RNN = 512
RG_BLOCKS = 8
RG_BW = D_RNN // RG_BLOCKS
CONV_W = 4
RG_C = 8.0

ODD_SPLITS = (Q_LORA, KV_LORA, C_ROPE, D_RNN, D_RNN)
ODD_IN = sum(ODD_SPLITS)
ODD_MIX = C_HEADS * C_V + D_RNN

D_FF = 3584
N_EXPERTS = 8
TOP_K = 2
D_FF_EXPERT = 3584
MOE_BLOCK = 128

kernel_name = 'hybrid_dsa_dilated_mla_rglru_moe'


def rms_norm(x, g):
    xf = x.astype(jnp.float32)
    y = xf * lax.rsqrt(jnp.mean(xf * xf, axis=-1, keepdims=True) + NORM_EPS)
    return (y * g.astype(jnp.float32)).astype(x.dtype)


def apply_rope(x, rot_dim):
    s = x.shape[1]
    half = rot_dim // 2
    inv_freq = 1.0 / (ROPE_THETA ** (jnp.arange(half, dtype=jnp.float32) * (2.0 / rot_dim)))
    ang = jnp.arange(s, dtype=jnp.float32)[:, None] * inv_freq[None, :]
    bshape = (1, s) + (1,) * (x.ndim - 3) + (half,)
    cos = jnp.cos(ang).reshape(bshape)
    sin = jnp.sin(ang).reshape(bshape)
    x1 = x[..., :half].astype(jnp.float32)
    x2 = x[..., half:rot_dim].astype(jnp.float32)
    rot = jnp.concatenate([x1 * cos - x2 * sin, x1 * sin + x2 * cos], axis=-1).astype(x.dtype)
    return jnp.concatenate([rot, x[..., rot_dim:]], axis=-1)


def split_cols(a, sizes):
    return jnp.split(a, [int(c) for c in np.cumsum(sizes)[:-1]], axis=-1)


def swiglu(x, w_gate, w_up, w_down):
    return (jax.nn.silu(x @ w_gate) * (x @ w_up)) @ w_down


def over_query_blocks(fn, b, s):
    out = lax.map(fn, jnp.arange(s // BLOCK_Q))
    return jnp.swapaxes(out, 0, 1).reshape(b, s, out.shape[-1])


def dsa_attention(q, k, v, qi, ki, wi):
    b, s = q.shape[:2]
    topk = min(TOPK_MAX, s // 4)
    group = A_HEADS // A_KV_HEADS
    scale = HEAD_DIM ** -0.5
    gather = jax.vmap(lambda arr, idx: arr[idx])
    key_pos = jnp.arange(s)
    ki32 = ki.astype(jnp.float32)

    def block(i):
        t0 = i * BLOCK_Q
        qpos = t0 + jnp.arange(BLOCK_Q)
        qb = lax.dynamic_slice_in_dim(q, t0, BLOCK_Q, axis=1)
        qib = lax.dynamic_slice_in_dim(qi, t0, BLOCK_Q, axis=1).astype(jnp.float32)
        wib = lax.dynamic_slice_in_dim(wi, t0, BLOCK_Q, axis=1).astype(jnp.float32)
        logit = jnp.einsum('bqhd,bsd->bqhs', qib, ki32) * (IDX_DIM ** -0.5)
        score = jnp.einsum('bqh,bqhs->bqs', wib, jax.nn.relu(logit))
        causal = key_pos[None, :] <= qpos[:, None]
        score = jnp.where(causal[None], score, -jnp.inf)
        _, idx = lax.top_k(score, topk)
        valid = idx <= qpos[None, :, None]
        kg = gather(k, idx)
        vg = gather(v, idx)
        qg = qb.reshape(b, BLOCK_Q, A_KV_HEADS, group, HEAD_DIM)
        sc = jnp.einsum('bqngd,bqknd->bngqk', qg, kg).astype(jnp.float32) * scale
        sc = jnp.where(valid[:, None, None], sc, -jnp.inf)
        p = jax.nn.softmax(sc, axis=-1).astype(v.dtype)
        o = jnp.einsum('bngqk,bqknd->bqngd', p, vg)
        return o.reshape(b, BLOCK_Q, A_HEADS * HEAD_DIM)

    return over_query_blocks(block, b, s)


def dilated_attention(q, k, v):
    b, s, h, dh = q.shape
    scale = dh ** -0.5
    strided = []
    for (w, d) in DILATED_PATTERNS:
        pad = ((0, 0), (w, 0), (0, 0), (0, 0))
        kp = jnp.pad(k, pad).reshape(b, (s + w) // d, d, h, dh)
        vp = jnp.pad(v, pad).reshape(b, (s + w) // d, d, h, dh)
        strided.append((kp, vp))

    def block(i):
        t0 = i * BLOCK_Q
        qb = lax.dynamic_slice_in_dim(q, t0, BLOCK_Q, axis=1)
        outs, lses = [], []
        for (w, d), (kp, vp) in zip(DILATED_PATTERNS, strided):
            n_off = w // d
            nq = BLOCK_Q // d
            band = n_off + nq
            qs = qb.reshape(b, nq, d, h, dh)
            ks = lax.dynamic_slice_in_dim(kp, t0 // d, band, axis=1)
            vs = lax.dynamic_slice_in_dim(vp, t0 // d, band, axis=1)
            sc = jnp.einsum('bmchd,buchd->bhcmu', qs, ks).astype(jnp.float32) * scale
            m_idx = jnp.arange(nq)[:, None]
            u_idx = jnp.arange(band)[None, :]
            j = m_idx + n_off - u_idx
            key_pos = t0 + (u_idx - n_off) * d + jnp.arange(d)[:, None]
            mask = ((j >= 0) & (j <= n_off))[None] & (key_pos >= 0)[:, None, :]
            sc = jnp.where(mask[None, None], sc, -jnp.inf)
            lse = jax.nn.logsumexp(sc, axis=-1)
            p = jnp.exp(sc - lse[..., None]).astype(v.dtype)
            o = jnp.einsum('bhcmu,buchd->bmchd', p, vs).reshape(b, BLOCK_Q, h, dh)
            outs.append(o)
            lses.append(jnp.transpose(lse, (0, 3, 2, 1)).reshape(b, BLOCK_Q, h))
        wts = jax.nn.softmax(jnp.stack(lses, axis=-1), axis=-1).astype(v.dtype)
        o = jnp.einsum('bqhr,rbqhd->bqhd', wts, jnp.stack(outs, axis=0))
        return o.reshape(b, BLOCK_Q, h * dh)

    return over_query_blocks(block, b, s)


def mla_attention(c_q, c_kv, k_rope, q_norm, w_uq, kv_norm, w_ukv):
    b, s = c_q.shape[:2]
    q = (rms_norm(c_q, q_norm) @ w_uq).reshape(b, s, C_HEADS, C_NOPE + C_ROPE)
    q_nope, q_rope = q[..., :C_NOPE], apply_rope(q[..., C_NOPE:], C_ROPE)
    kv = (rms_norm(c_kv, kv_norm) @ w_ukv).reshape(b, s, C_HEADS, C_NOPE + C_V)
    k_nope, v = kv[..., :C_NOPE], kv[..., C_NOPE:]
    k_rope = apply_rope(k_rope, C_ROPE)
    scale = (C_NOPE + C_ROPE) ** -0.5
    key_pos = jnp.arange(s)

    def block(i):
        t0 = i * BLOCK_Q
        qpos = t0 + jnp.arange(BLOCK_Q)
        qn = lax.dynamic_slice_in_dim(q_nope, t0, BLOCK_Q, axis=1)
        qr = lax.dynamic_slice_in_dim(q_rope, t0, BLOCK_Q, axis=1)
        sc = (jnp.einsum('bqhd,bshd->bhqs', qn, k_nope)
              + jnp.einsum('bqhd,bsd->bhqs', qr, k_rope)).astype(jnp.float32) * scale
        causal = key_pos[None, :] <= qpos[:, None]
        sc = jnp.where(causal[None, None], sc, -jnp.inf)
        p = jax.nn.softmax(sc, axis=-1).astype(v.dtype)
        o = jnp.einsum('bhqs,bshd->bqhd', p, v)
        return o.reshape(b, BLOCK_Q, C_HEADS * C_V)

    return over_query_blocks(block, b, s)


def rglru_mixer(xr, gate, conv_w, conv_b, w_a, b_a, w_x, b_x, lam):
    b, s, _ = xr.shape
    xc = lax.conv_general_dilated(xr, conv_w[:, None, :], window_strides=(1,),
                                  padding=((CONV_W - 1, 0),),
                                  dimension_numbers=('NWC', 'WIO', 'NWC'),
                                  feature_group_count=D_RNN)
    xc = (xc + conv_b).astype(jnp.float32)
    xb = xc.reshape(b, s, RG_BLOCKS, RG_BW)
    r = jax.nn.sigmoid(jnp.einsum('bsni,nij->bsnj', xb, w_a.astype(jnp.float32)).reshape(b, s, D_RNN)
                       + b_a.astype(jnp.float32))
    i_g = jax.nn.sigmoid(jnp.einsum('bsni,nij->bsnj', xb, w_x.astype(jnp.float32)).reshape(b, s, D_RNN)
                         + b_x.astype(jnp.float32))
    log_a = -RG_C * jax.nn.softplus(-lam.astype(jnp.float32)) * r
    a = jnp.exp(log_a)
    u = jnp.sqrt(-jnp.expm1(2.0 * log_a)) * (i_g * xc)

    def combine(left, right):
        a_l, b_l = left
        a_r, b_r = right
        return a_l * a_r, a_r * b_l + b_r

    _, h = lax.associative_scan(combine, (a, u), axis=1)
    return (h * jax.nn.gelu(gate.astype(jnp.float32))).astype(xr.dtype)


def moe_swiglu(x, router, w_gate, w_up, w_down):
    b, s, dm = x.shape
    n = b * s
    xf = x.reshape(n, dm)
    logits = (xf @ router).astype(jnp.float32)
    top_val, top_idx = lax.top_k(logits, TOP_K)
    gates = jax.nn.softmax(top_val, axis=-1)
    e_flat = top_idx.reshape(-1)
    g_flat = gates.reshape(-1)
    tok_flat = jnp.repeat(jnp.arange(n, dtype=jnp.int32), TOP_K)
    counts = jnp.bincount(e_flat, length=N_EXPERTS)
    padded = (counts + MOE_BLOCK - 1) // MOE_BLOCK * MOE_BLOCK
    pad_end = jnp.cumsum(padded)
    pad_start = pad_end - padded
    cnt_start = jnp.cumsum(counts) - counts
    order = jnp.argsort(e_flat)
    e_sorted = e_flat[order]
    rank = jnp.arange(n * TOP_K) - cnt_start[e_sorted]
    dest = pad_start[e_sorted] + rank
    n_rows = n * TOP_K + N_EXPERTS * MOE_BLOCK
    row_tok = jnp.full((n_rows,), n, jnp.int32).at[dest].set(tok_flat[order])
    row_gate = jnp.zeros((n_rows,), jnp.float32).at[dest].set(g_flat[order])
    n_blk = n_rows // MOE_BLOCK
    blk_expert = jnp.clip(jnp.searchsorted(pad_end, jnp.arange(n_blk) * MOE_BLOCK, side='right'),
                          0, N_EXPERTS - 1)
    x_rows = jnp.concatenate([xf, jnp.zeros((1, dm), xf.dtype)], axis=0)[row_tok]
    x_rows = x_rows.reshape(n_blk, MOE_BLOCK, dm)

    def expert_block(args):
        xb, e = args
        return swiglu(xb, w_gate[e], w_up[e], w_down[e])

    y_rows = lax.map(expert_block, (x_rows, blk_expert)).reshape(n_rows, dm)
    y_rows = y_rows * row_gate[:, None].astype(y_rows.dtype)
    y = jax.ops.segment_sum(y_rows, row_tok, num_segments=n + 1)[:n]
    return y.reshape(b, s, dm)


def setup_inputs(seed: int = 0) -> dict:
    key = jax.random.key(seed)
    ks = iter(jax.random.split(key, 40))
    res = (2.0 * DEPTH) ** -0.5

    def nrm(shape, fan_in, extra=1.0):
        return jax.random.normal(next(ks), shape, jnp.float32) * (extra * fan_in ** -0.5)

    def gain(shape):
        return 1.0 + 0.05 * jax.random.normal(next(ks), shape, jnp.float32)

    def bias(shape):
        return 0.01 * jax.random.normal(next(ks), shape, jnp.float32)

    x = jax.random.normal(next(ks), (BATCH, SEQ, D_MODEL), jnp.float32)
    u = jax.random.uniform(next(ks), (N_ODD, D_RNN), jnp.float32, minval=0.9, maxval=0.999)
    a0 = u ** (1.0 / RG_C)
    rg_lambda = jnp.log(a0) - jnp.log1p(-a0)
    return {
        'x': x,
        'ev_norm_mix': gain((N_EVEN, D_MODEL)),
        'ev_w_in': nrm((N_EVEN, D_MODEL, EVEN_IN), D_MODEL),
        'ev_w_out': nrm((N_EVEN, EVEN_MIX, D_MODEL), EVEN_MIX, res),
        'ev_norm_ffn': gain((N_EVEN, D_MODEL)),
        'ffn_w_gate': nrm((N_EVEN, D_MODEL, D_FF), D_MODEL),
        'ffn_w_up': nrm((N_EVEN, D_MODEL, D_FF), D_MODEL),
        'ffn_w_down': nrm((N_EVEN, D_FF, D_MODEL), D_FF, res),
        'od_norm_mix': gain((N_ODD, D_MODEL)),
        'od_w_in': nrm((N_ODD, D_MODEL, ODD_IN), D_MODEL),
        'mla_q_norm': gain((N_ODD, Q_LORA)),
        'mla_w_uq': nrm((N_ODD, Q_LORA, C_HEADS * (C_NOPE + C_ROPE)), Q_LORA),
        'mla_kv_norm': gain((N_ODD, KV_LORA)),
        'mla_w_ukv': nrm((N_ODD, KV_LORA, C_HEADS * (C_NOPE + C_V)), KV_LORA),
        'rg_conv_w': nrm((N_ODD, CONV_W, D_RNN), CONV_W),
        'rg_conv_b': bias((N_ODD, D_RNN)),
        'rg_w_a': nrm((N_ODD, RG_BLOCKS, RG_BW, RG_BW), RG_BW),
        'rg_b_a': bias((N_ODD, D_RNN)),
        'rg_w_x': nrm((N_ODD, RG_BLOCKS, RG_BW, RG_BW), RG_BW),
        'rg_b_x': bias((N_ODD, D_RNN)),
        'rg_lambda': rg_lambda,
        'od_w_out': nrm((N_ODD, ODD_MIX, D_MODEL), ODD_MIX, res),
        'od_norm_ffn': gain((N_ODD, D_MODEL)),
        'moe_router': nrm((N_ODD, D_MODEL, N_EXPERTS), D_MODEL),
        'moe_w_gate': nrm((N_ODD, N_EXPERTS, D_MODEL, D_FF_EXPERT), D_MODEL),
        'moe_w_up': nrm((N_ODD, N_EXPERTS, D_MODEL, D_FF_EXPERT), D_MODEL),
        'moe_w_down': nrm((N_ODD, N_EXPERTS, D_FF_EXPERT, D_MODEL), D_FF_EXPERT, res),
        'final_norm': gain((D_MODEL,)),
    }


def reference(x, ev_norm_mix, ev_w_in, ev_w_out, ev_norm_ffn, ffn_w_gate, ffn_w_up, ffn_w_down,
              od_norm_mix, od_w_in, mla_q_norm, mla_w_uq, mla_kv_norm, mla_w_ukv,
              rg_conv_w, rg_conv_b, rg_w_a, rg_b_a, rg_w_x, rg_b_x, rg_lambda,
              od_w_out, od_norm_ffn, moe_router, moe_w_gate, moe_w_up, moe_w_down, final_norm):
    b, s, _ = x.shape
    for layer in range(DEPTH):
        i = layer // 2
        if layer % 2 == 0:
            h = rms_norm(x, ev_norm_mix[i])
            qa, ka, va, qi, ki, wi, qb, kb, vb = split_cols(h @ ev_w_in[i], EVEN_SPLITS)
            qa = apply_rope(qa.reshape(b, s, A_HEADS, HEAD_DIM), ROT_DIM)
            ka = apply_rope(ka.reshape(b, s, A_KV_HEADS, HEAD_DIM), ROT_DIM)
            va = va.reshape(b, s, A_KV_HEADS, HEAD_DIM)
            qi = apply_rope(qi.reshape(b, s, IDX_HEADS, IDX_DIM), IDX_ROT)
            ki = apply_rope(ki, IDX_ROT)
            wi = wi * (IDX_HEADS ** -0.5)
            out_a = dsa_attention(qa, ka, va, qi, ki, wi)
            qb = apply_rope(qb.reshape(b, s, B_HEADS, HEAD_DIM), ROT_DIM)
            kb = apply_rope(kb.reshape(b, s, B_HEADS, HEAD_DIM), ROT_DIM)
            vb = vb.reshape(b, s, B_HEADS, HEAD_DIM)
            out_b = dilated_attention(qb, kb, vb)
            x = x + jnp.concatenate([out_a, out_b], axis=-1) @ ev_w_out[i]
            x = x + swiglu(rms_norm(x, ev_norm_ffn[i]), ffn_w_gate[i], ffn_w_up[i], ffn_w_down[i])
        else:
            h = rms_norm(x, od_norm_mix[i])
            c_q, c_kv, k_rope, x_rnn, g_rnn = split_cols(h @ od_w_in[i], ODD_SPLITS)
            out_c = mla_attention(c_q, c_kv, k_rope, mla_q_norm[i], mla_w_uq[i],
                                  mla_kv_norm[i], mla_w_ukv[i])
            out_d = rglru_mixer(x_rnn, g_rnn, rg_conv_w[i], rg_conv_b[i], rg_w_a[i], rg_b_a[i],
                                rg_w_x[i], rg_b_x[i], rg_lambda[i])
            x = x + jnp.concatenate([out_c, out_d], axis=-1) @ od_w_out[i]
            x = x + moe_swiglu(rms_norm(x, od_norm_ffn[i]), moe_router[i], moe_w_gate[i],
                               moe_w_up[i], moe_w_down[i])
    return rms_norm(x, final_norm)
```

```python
import functools

import jax
import jax.numpy as jnp
import numpy as np
from jax import lax
from jax.experimental import pallas as pl
from jax.experimental.pallas import tpu as pltpu

D_MODEL = 1024
HEAD_DIM = 64
ROT_DIM = HEAD_DIM // 4
ROPE_THETA = 500000.0
NORM_EPS = 1e-6

A_HEADS = 8
A_KV_HEADS = 2
IDX_HEADS = 8
IDX_DIM = 32
IDX_ROT = IDX_DIM // 4
TOPK_MAX = 256

B_HEADS = 8
DILATED_PATTERNS = ((128, 1), (512, 4), (2048, 16))

C_HEADS = 8
C_NOPE = 64
C_ROPE = 32
C_V = 64
Q_LORA = 256
KV_LORA = 128

D_RNN = 512
RG_BLOCKS = 8
RG_BW = D_RNN // RG_BLOCKS
CONV_W = 4
RG_C = 8.0

D_FF = 3584
N_EXPERTS = 8
TOP_K = 2

LANES = 128
VMEM_LIMIT_BYTES = 56 * 1024 * 1024
BLOCK_Q = 128
KEY_CHUNK = 512
ROW_TILE = 512
FFN_ROW_TILE = 1024
FFN_COL_TILE = 512
MOE_ROW_TILE = 1024
GATHER_ROWS = 256
RG_CHUNK = 256
NEG_BIG = -1e30
INT_MIN = -2147483648

_NT = (((1,), (1,)), ((), ()))


def _params(semantics, vmem=VMEM_LIMIT_BYTES):
    return pltpu.CompilerParams(dimension_semantics=semantics, vmem_limit_bytes=vmem)


def _bf16(a):
    return a if a.dtype == jnp.bfloat16 else a.astype(jnp.bfloat16)


def _rmsnorm_kernel(x_ref, g_ref, o_ref):
    x = x_ref[...].astype(jnp.float32)
    ms = jnp.mean(x * x, axis=-1, keepdims=True)
    o_ref[...] = (x * lax.rsqrt(ms + NORM_EPS) * g_ref[...]).astype(o_ref.dtype)


def _rmsnorm(x, g, *, width=None, col_block=0, out_dtype=jnp.bfloat16, name="rmsnorm"):
    n = x.shape[0]
    width = x.shape[1] if width is None else width
    return pl.pallas_call(
        _rmsnorm_kernel,
        grid=(n // ROW_TILE,),
        in_specs=[pl.BlockSpec((ROW_TILE, width), lambda i: (i, col_block)),
                  pl.BlockSpec((1, width), lambda i: (0, 0))],
        out_specs=pl.BlockSpec((ROW_TILE, width), lambda i: (i, 0)),
        out_shape=jax.ShapeDtypeStruct((n, width), out_dtype),
        compiler_params=_params(("parallel",)),
        name=name,
    )(x, g.reshape(1, width).astype(jnp.float32))


def _rope_tables(seq, period, rot_dim):
    half = rot_dim // 2
    inv_freq = 1.0 / (ROPE_THETA ** (jnp.arange(half, dtype=jnp.float32) * (2.0 / rot_dim)))
    ang = jnp.arange(seq, dtype=jnp.float32)[:, None] * inv_freq[None, :]
    cos, sin = jnp.cos(ang), jnp.sin(ang)
    rest = period - rot_dim
    ones = jnp.ones((seq, rest), jnp.float32)
    zeros = jnp.zeros((seq, rest), jnp.float32)
    zh = jnp.zeros((seq, half), jnp.float32)
    c = jnp.concatenate([cos, cos, ones], axis=1)
    sa = jnp.concatenate([-sin, zh, zeros], axis=1)
    sb = jnp.concatenate([zh, sin, zeros], axis=1)
    reps = LANES // period
    return tuple(jnp.tile(t, (1, reps)) for t in (c, sa, sb))


def _proj_kernel(*refs, rope_half, has_res):
    a_ref, w_ref = refs[0], refs[1]
    o_ref = refs[-1]
    y = jnp.dot(_bf16(a_ref[...]), _bf16(w_ref[...]), preferred_element_type=jnp.float32)
    pos = 2
    if rope_half:
        c_ref, sa_ref, sb_ref = refs[2:5]
        pos = 5
        tn = y.shape[1]
        reps = tn // LANES
        c = jnp.tile(c_ref[...], (1, reps))
        sa = jnp.tile(sa_ref[...], (1, reps))
        sb = jnp.tile(sb_ref[...], (1, reps))
        y = y * c + pltpu.roll(y, tn - rope_half, 1) * sa + pltpu.roll(y, rope_half, 1) * sb
    if has_res:
        y = y + refs[pos][...]
    o_ref[...] = y.astype(o_ref.dtype)


def _proj(a, w, *, seq, rope=None, rope_half=0, res=None, out_dtype=jnp.float32, name="proj"):
    n, k = a.shape
    np_ = w.shape[1]
    tm = ROW_TILE
    seq_blocks = seq // tm
    in_specs = [pl.BlockSpec((tm, k), lambda i: (i, 0)),
                pl.BlockSpec((k, np_), lambda i: (0, 0))]
    args = [a, w]
    if rope is not None:
        for t in rope:
            in_specs.append(pl.BlockSpec((tm, LANES), lambda i: (i % seq_blocks, 0)))
            args.append(t)
    if res is not None:
        in_specs.append(pl.BlockSpec((tm, np_), lambda i: (i, 0)))
        args.append(res)
    return pl.pallas_call(
        functools.partial(_proj_kernel, rope_half=rope_half if rope is not None else 0, has_res=res is not None),
        grid=(n // tm,),
        in_specs=in_specs,
        out_specs=pl.BlockSpec((tm, np_), lambda i: (i, 0)),
        out_shape=jax.ShapeDtypeStruct((n, np_), out_dtype),
        compiler_params=_params(("parallel",)),
        name=name,
    )(*args)


def _silu(g):
    return g * (1.0 / (1.0 + jnp.exp(-g)))


def _ffn_kernel(xn_ref, wg_ref, wu_ref, wd_ref, res_ref, o_ref, acc_ref):
    f = pl.program_id(1)

    @pl.when(f == 0)
    def _():
        acc_ref[...] = jnp.zeros_like(acc_ref)

    xn = xn_ref[...]
    g = jnp.dot(xn, _bf16(wg_ref[...]), preferred_element_type=jnp.float32)
    u = jnp.dot(xn, _bf16(wu_ref[...]), preferred_element_type=jnp.float32)
    h = (_silu(g) * u).astype(jnp.bfloat16)
    acc_ref[...] += jnp.dot(h, _bf16(wd_ref[...]), preferred_element_type=jnp.float32)

    @pl.when(f == pl.num_programs(1) - 1)
    def _():
        o_ref[...] = res_ref[...] + acc_ref[...]


def _ffn(xn, w_gate, w_up, w_down, res):
    n, d = xn.shape
    ff = w_gate.shape[1]
    tm, tf = FFN_ROW_TILE, FFN_COL_TILE
    return pl.pallas_call(
        _ffn_kernel,
        grid=(n // tm, ff // tf),
        in_specs=[pl.BlockSpec((tm, d), lambda i, f: (i, 0)),
                  pl.BlockSpec((d, tf), lambda i, f: (0, f)),
                  pl.BlockSpec((d, tf), lambda i, f: (0, f)),
                  pl.BlockSpec((tf, d), lambda i, f: (f, 0)),
                  pl.BlockSpec((tm, d), lambda i, f: (i, 0))],
        out_specs=pl.BlockSpec((tm, d), lambda i, f: (i, 0)),
        out_shape=jax.ShapeDtypeStruct((n, d), jnp.float32),
        scratch_shapes=[pltpu.VMEM((tm, d), jnp.float32)],
        compiler_params=_params(("parallel", "arbitrary")),
        name="ffn_swiglu",
    )(xn, w_gate, w_up, w_down, res)


def _dsa_kernel(q_ref, k_ref, v_ref, qi_ref, ki_ref, wi_ref, o_ref, key_scr, bias_scr, *, topk):
    i = pl.program_id(1)
    n_chunks = (i * BLOCK_Q + BLOCK_Q + KEY_CHUNK - 1) // KEY_CHUNK
    qpos = i * BLOCK_Q + lax.broadcasted_iota(jnp.int32, (BLOCK_Q, 1), 0)
    kloc = lax.broadcasted_iota(jnp.int32, (1, KEY_CHUNK), 1)

    qi = _bf16(qi_ref[0])
    w = wi_ref[0][:, :IDX_HEADS] * (IDX_DIM ** -0.5 * IDX_HEADS ** -0.5)

    def index_body(c, carry):
        kc = _bf16(ki_ref[0, pl.ds(c * KEY_CHUNK, KEY_CHUNK), :])[:, :IDX_DIM]
        score = jnp.zeros((BLOCK_Q, KEY_CHUNK), jnp.float32)
        for h in range(IDX_HEADS):
            lg = lax.dot_general(qi[:, h * IDX_DIM:(h + 1) * IDX_DIM], kc, _NT,
                                 preferred_element_type=jnp.float32)
            score = score + w[:, h:h + 1] * jnp.maximum(lg, 0.0)
        bits = pltpu.bitcast(score, jnp.int32)
        key = jnp.where(bits < 0, bits ^ jnp.int32(0x7FFFFFFF), bits)
        key_scr[c] = jnp.where(c * KEY_CHUNK + kloc <= qpos, key, jnp.int32(INT_MIN))
        return carry

    lax.fori_loop(0, n_chunks, index_body, 0)

    def count(pred):
        def body(c, acc):
            return acc + pred(key_scr[c]).astype(jnp.int32)
        acc = lax.fori_loop(0, n_chunks, body, jnp.zeros((BLOCK_Q, KEY_CHUNK), jnp.int32))
        return jnp.sum(acc, axis=1, keepdims=True)

    def bit_body(t, res):
        cand = res | jnp.left_shift(jnp.int32(1), 31 - t)
        cand_key = cand ^ jnp.int32(INT_MIN)
        return jnp.where(count(lambda key: key >= cand_key) >= topk, cand, res)

    thr = lax.fori_loop(0, 32, bit_body, jnp.zeros((BLOCK_Q, 1), jnp.int32)) ^ jnp.int32(INT_MIN)
    need = (topk - count(lambda key: key > thr)).astype(jnp.float32)

    tri = (lax.broadcasted_iota(jnp.int32, (KEY_CHUNK, KEY_CHUNK), 0)
           <= lax.broadcasted_iota(jnp.int32, (KEY_CHUNK, KEY_CHUNK), 1)).astype(jnp.bfloat16)

    def mask_body(c, seen):
        key = key_scr[c]
        eq = key == thr
        running = jnp.dot(eq.astype(jnp.bfloat16), tri, preferred_element_type=jnp.float32) + seen
        sel = ((key > thr) | (eq & (running <= need))) & (c * KEY_CHUNK + kloc <= qpos)
        bias_scr[c] = jnp.where(sel, 0.0, NEG_BIG)
        return running[:, KEY_CHUNK - 1:KEY_CHUNK]

    lax.fori_loop(0, n_chunks, mask_body, jnp.zeros((BLOCK_Q, 1), jnp.float32))

    q = _bf16(q_ref[0])
    scale = HEAD_DIM ** -0.5
    group = A_HEADS // A_KV_HEADS
    for n in range(A_KV_HEADS):
        def att_body(c, carry, n=n):
            rows = pl.ds(c * KEY_CHUNK, KEY_CHUNK)
            kc = _bf16(k_ref[0, rows, :])[:, n * HEAD_DIM:(n + 1) * HEAD_DIM]
            vc = _bf16(v_ref[0, rows, :])[:, n * HEAD_DIM:(n + 1) * HEAD_DIM]
            bias = bias_scr[c]
            out = []
            for g in range(group):
                m, l, acc = carry[g]
                h = n * group + g
                s = lax.dot_general(q[:, h * HEAD_DIM:(h + 1) * HEAD_DIM], kc, _NT,
                                    preferred_element_type=jnp.float32) * scale + bias
                m_new = jnp.maximum(m, jnp.max(s, axis=1, keepdims=True))
                alpha = jnp.exp(m - m_new)
                p = jnp.exp(s - m_new)
                l = l * alpha + jnp.sum(p, axis=1, keepdims=True)
                acc = acc * alpha + jnp.dot(p.astype(jnp.bfloat16), vc, preferred_element_type=jnp.float32)
                out.append((m_new, l, acc))
            return tuple(out)

        init = tuple((jnp.full((BLOCK_Q, 1), NEG_BIG, jnp.float32),
                      jnp.zeros((BLOCK_Q, 1), jnp.float32),
                      jnp.zeros((BLOCK_Q, HEAD_DIM), jnp.float32)) for _ in range(group))
        state = lax.fori_loop(0, n_chunks, att_body, init)
        for g in range(group):
            h = n * group + g
            _, l, acc = state[g]
            o_ref[0, :, h * HEAD_DIM:(h + 1) * HEAD_DIM] = acc * (1.0 / l)


def _dsa(p64, pv, pi, *, batch, seq):
    topk = min(TOPK_MAX, seq // 4)
    n_chunks = seq // KEY_CHUNK
    qw = A_HEADS * HEAD_DIM
    return pl.pallas_call(
        functools.partial(_dsa_kernel, topk=topk),
        grid=(batch, seq // BLOCK_Q),
        in_specs=[pl.BlockSpec((1, BLOCK_Q, qw), lambda b, i: (b, i, 0)),
                  pl.BlockSpec((1, seq, LANES), lambda b, i: (b, 0, 12)),
                  pl.BlockSpec((1, seq, LANES), lambda b, i: (b, 0, 4)),
                  pl.BlockSpec((1, BLOCK_Q, IDX_HEADS * IDX_DIM), lambda b, i: (b, i, 0)),
                  pl.BlockSpec((1, seq, LANES), lambda b, i: (b, 0, 2)),
                  pl.BlockSpec((1, BLOCK_Q, LANES), lambda b, i: (b, i, 5))],
        out_specs=pl.BlockSpec((1, BLOCK_Q, qw), lambda b, i: (b, i, 0)),
        out_shape=jax.ShapeDtypeStruct((batch, seq, qw), jnp.float32),
        scratch_shapes=[pltpu.VMEM((n_chunks, BLOCK_Q, KEY_CHUNK), jnp.int32),
                        pltpu.VMEM((n_chunks, BLOCK_Q, KEY_CHUNK), jnp.float32)],
        compiler_params=_params(("parallel", "arbitrary")),
        name="dsa_attention",
    )(p64, p64, pv, pi, pi, pv)


def _dilated_kernel(q_ref, k_ref, v_ref, o_ref, acc_scr, m_scr, l_scr, *, seq):
    scale = HEAD_DIM ** -0.5
    row = lax.broadcasted_iota(jnp.int32, (BLOCK_Q, BLOCK_Q), 0)
    col = lax.broadcasted_iota(jnp.int32, (BLOCK_Q, BLOCK_Q), 1)
    mask_cur = col <= row
    mask_prev = col >= row

    acc_scr[...] = jnp.zeros_like(acc_scr)
    l_scr[...] = jnp.zeros_like(l_scr)
    m_scr[...] = jnp.full_like(m_scr, NEG_BIG)

    def spread(cols):
        return jnp.concatenate([jnp.broadcast_to(c, (BLOCK_Q, HEAD_DIM)) for c in cols], axis=1)

    for (window, dil) in DILATED_PATTERNS:
        assert window // dil == BLOCK_Q and seq % (dil * BLOCK_Q) == 0
        n_blocks = seq // (dil * BLOCK_Q)

        def block_body(idx, carry, dil=dil, n_blocks=n_blocks):
            phase = idx // n_blocks
            blk = idx % n_blocks
            rows = pl.ds(phase + dil * BLOCK_Q * blk, BLOCK_Q, stride=dil)
            q2 = _bf16(q_ref[0, rows, :])
            k_cur = _bf16(k_ref[0, rows, :])
            v_cur = _bf16(v_ref[0, rows, :])
            if n_blocks > 1:
                rows_p = pl.ds(phase + dil * BLOCK_Q * jnp.maximum(blk - 1, 0), BLOCK_Q, stride=dil)
                k_prev = _bf16(k_ref[0, rows_p, :])
                v_prev = _bf16(v_ref[0, rows_p, :])
                mask_p = mask_prev & (blk > 0)
            m_old = m_scr[rows, :]
            new_m, new_sum, new_pv = [], [], []
            for h in range(2):
                sl = slice(h * HEAD_DIM, (h + 1) * HEAD_DIM)
                s_c = lax.dot_general(q2[:, sl], k_cur[:, sl], _NT, preferred_element_type=jnp.float32) * scale
                s_c = jnp.where(mask_cur, s_c, NEG_BIG)
                mx = jnp.max(s_c, axis=1, keepdims=True)
                if n_blocks > 1:
                    s_p = lax.dot_general(q2[:, sl], k_prev[:, sl], _NT, preferred_element_type=jnp.float32) * scale
                    s_p = jnp.where(mask_p, s_p, NEG_BIG)
                    mx = jnp.maximum(mx, jnp.max(s_p, axis=1, keepdims=True))
                m_h = jnp.maximum(m_old[:, h * HEAD_DIM:h * HEAD_DIM + 1], mx)
                p_c = jnp.exp(s_c - m_h)
                psum = jnp.sum(p_c, axis=1, keepdims=True)
                pv = jnp.dot(p_c.astype(jnp.bfloat16), v_cur[:, sl], preferred_element_type=jnp.float32)
                if n_blocks > 1:
                    p_p = jnp.exp(s_p - m_h)
                    psum = psum + jnp.sum(p_p, axis=1, keepdims=True)
                    pv = pv + jnp.dot(p_p.astype(jnp.bfloat16), v_prev[:, sl], preferred_element_type=jnp.float32)
                new_m.append(m_h)
                new_sum.append(psum)
                new_pv.append(pv)
            m_new = spread(new_m)
            alpha = jnp.exp(m_old - m_new)
            m_scr[rows, :] = m_new
            l_scr[rows, :] = l_scr[rows, :] * alpha + spread(new_sum)
            acc_scr[rows, :] = acc_scr[rows, :] * alpha + jnp.concatenate(new_pv, axis=1)
            return carry

        lax.fori_loop(0, dil * n_blocks, block_body, 0)

    o_ref[0] = acc_scr[...] * (1.0 / l_scr[...])


def _dilated(p64, pv, *, batch, seq):
    pairs = B_HEADS // 2
    return pl.pallas_call(
        functools.partial(_dilated_kernel, seq=seq),
        grid=(batch, pairs),
        in_specs=[pl.BlockSpec((1, seq, LANES), lambda b, p: (b, 0, 4 + p)),
                  pl.BlockSpec((1, seq, LANES), lambda b, p: (b, 0, 8 + p)),
                  pl.BlockSpec((1, seq, LANES), lambda b, p: (b, 0, p))],
        out_specs=pl.BlockSpec((1, seq, LANES), lambda b, p: (b, 0, p)),
        out_shape=jax.ShapeDtypeStruct((batch, seq, B_HEADS * HEAD_DIM), jnp.float32),
        scratch_shapes=[pltpu.VMEM((seq, LANES), jnp.float32)] * 3,
        compiler_params=_params(("parallel", "parallel")),
        name="dilated_attention",
    )(p64, p64, pv)


def _mla_kernel(qn_ref, qr_ref, kv_ref, v_ref, kr_ref, o_ref):
    i = pl.program_id(1)
    n_chunks = (i * BLOCK_Q + BLOCK_Q + KEY_CHUNK - 1) // KEY_CHUNK
    qpos = i * BLOCK_Q + lax.broadcasted_iota(jnp.int32, (BLOCK_Q, 1), 0)
    kloc = lax.broadcasted_iota(jnp.int32, (1, KEY_CHUNK), 1)
    scale = (C_NOPE + C_ROPE) ** -0.5
    qn = _bf16(qn_ref[0])
    qr = _bf16(qr_ref[0])
    for h in range(C_HEADS):
        qn_h = qn[:, h * C_NOPE:(h + 1) * C_NOPE]
        qr_h = qr[:, h * C_ROPE:(h + 1) * C_ROPE]

        def body(c, carry, h=h, qn_h=qn_h, qr_h=qr_h):
            m, l, acc = carry
            rows = pl.ds(c * KEY_CHUNK, KEY_CHUNK)
            kn = _bf16(kv_ref[0, rows, h * C_NOPE:(h + 1) * C_NOPE])
            kr = _bf16(kr_ref[0, rows, :])[:, :C_ROPE]
            vc = _bf16(v_ref[0, rows, h * C_V:(h + 1) * C_V])
            s = (lax.dot_general(qn_h, kn, _NT, preferred_element_type=jnp.float32)
                 + lax.dot_general(qr_h, kr, _NT, preferred_element_type=jnp.float32)) * scale
            s = jnp.where(c * KEY_CHUNK + kloc <= qpos, s, NEG_BIG)
            m_new = jnp.maximum(m, jnp.max(s, axis=1, keepdims=True))
            alpha = jnp.exp(m - m_new)
            p = jnp.exp(s - m_new)
            l = l * alpha + jnp.sum(p, axis=1, keepdims=True)
            acc = acc * alpha + jnp.dot(p.astype(jnp.bfloat16), vc, preferred_element_type=jnp.float32)
            return m_new, l, acc

        init = (jnp.full((BLOCK_Q, 1), NEG_BIG, jnp.float32), jnp.zeros((BLOCK_Q, 1), jnp.float32),
                jnp.zeros((BLOCK_Q, C_V), jnp.float32))
        _, l, acc = lax.fori_loop(0, n_chunks, body, init)
        o_ref[0, :, h * C_V:(h + 1) * C_V] = acc * (1.0 / l)


def _mla(qn, qr, kv, kr, *, batch, seq):
    kw = C_HEADS * C_NOPE
    vw = C_HEADS * C_V
    return pl.pallas_call(
        _mla_kernel,
        grid=(batch, seq // BLOCK_Q),
        in_specs=[pl.BlockSpec((1, BLOCK_Q, kw), lambda b, i: (b, i, 0)),
                  pl.BlockSpec((1, BLOCK_Q, C_HEADS * C_ROPE), lambda b, i: (b, i, 0)),
                  pl.BlockSpec((1, seq, kw), lambda b, i: (b, 0, 0)),
                  pl.BlockSpec((1, seq, vw), lambda b, i: (b, 0, 1)),
                  pl.BlockSpec((1, seq, LANES), lambda b, i: (b, 0, 0))],
        out_specs=pl.BlockSpec((1, BLOCK_Q, vw), lambda b, i: (b, i, 0)),
        out_shape=jax.ShapeDtypeStruct((batch, seq, vw), jnp.float32),
        compiler_params=_params(("parallel", "arbitrary")),
        name="mla_attention",
    )(qn, qr, kv, kv, kr)


def _expm1(y):
    u = jnp.exp(y)
    safe = jnp.where(u == 1.0, 2.0, u)
    return jnp.where(u == 1.0, y, (u - 1.0) * y / jnp.log(safe))


def _gelu_tanh(x):
    return 0.5 * x * (1.0 + jnp.tanh(np.sqrt(2.0 / np.pi).astype(np.float32) * (x + 0.044715 * (x * x * x))))


def _rglru_kernel(x_ref, g_ref, cw_ref, cb_ref, wa_ref, ba_ref, wx_ref, bx_ref, lam_ref, o_ref, a_scr, u_scr, *, seq):
    cw = cw_ref[...]
    neg_c_softplus = -RG_C * jnp.log1p(jnp.exp(-jnp.abs(-lam_ref[...]))) - RG_C * jnp.maximum(-lam_ref[...], 0.0)
    hi = lax.Precision.HIGHEST
    sub = 8

    def chunk_body(ci, h):
        r0 = pl.multiple_of(ci * RG_CHUNK, RG_CHUNK)
        xa = x_ref[0, pl.ds(r0, RG_CHUNK), :]
        prev = x_ref[0, pl.ds(pl.multiple_of(jnp.maximum(r0 - sub, 0), sub), sub), :]
        prev = jnp.where(ci > 0, prev, 0.0)
        xcat = jnp.concatenate([prev, xa], axis=0)
        xc = cw[CONV_W - 1:CONV_W, :] * xa + cb_ref[...]
        for j in range(1, CONV_W):
            xc = xc + cw[CONV_W - 1 - j:CONV_W - j, :] * pltpu.roll(xcat, j, 0)[sub:, :]
        r = 1.0 / (1.0 + jnp.exp(-(jnp.dot(xc, wa_ref[...], precision=hi, preferred_element_type=jnp.float32)
                                   + ba_ref[...])))
        ig = 1.0 / (1.0 + jnp.exp(-(jnp.dot(xc, wx_ref[...], precision=hi, preferred_element_type=jnp.float32)
                                    + bx_ref[...])))
        log_a = neg_c_softplus * r
        a_scr[...] = jnp.exp(log_a)
        u_scr[...] = jnp.sqrt(-_expm1(2.0 * log_a)) * (ig * xc)
        gate = _gelu_tanh(g_ref[0, pl.ds(r0, RG_CHUNK), :])

        def group_body(gi, h):
            g0 = pl.multiple_of(gi * sub, sub)
            a8 = a_scr[pl.ds(g0, sub), :]
            u8 = u_scr[pl.ds(g0, sub), :]
            rows = []
            for t in range(sub):
                h = a8[t:t + 1, :] * h + u8[t:t + 1, :]
                rows.append(h)
            u_scr[pl.ds(g0, sub), :] = jnp.concatenate(rows, axis=0)
            return h

        h = lax.fori_loop(0, RG_CHUNK // sub, group_body, h)
        o_ref[0, pl.ds(r0, RG_CHUNK), :] = u_scr[...] * gate
        return h

    lax.fori_loop(0, seq // RG_CHUNK, chunk_body, jnp.zeros((1, D_RNN), jnp.float32))


def _block_diag(w):
    nb, bw, _ = w.shape
    eye = jnp.eye(nb, dtype=w.dtype)
    return (eye[:, None, :, None] * w[:, :, None, :]).reshape(nb * bw, nb * bw)


def _rglru(po, conv_w, conv_b, w_a, b_a, w_x, b_x, lam, *, batch, seq):
    row = lambda a: a.reshape(1, D_RNN).astype(jnp.float32)
    vec = pl.BlockSpec((1, D_RNN), lambda b: (0, 0))
    mat = pl.BlockSpec((D_RNN, D_RNN), lambda b: (0, 0))
    return pl.pallas_call(
        functools.partial(_rglru_kernel, seq=seq),
        grid=(batch,),
        in_specs=[pl.BlockSpec((1, seq, D_RNN), lambda b: (b, 0, 0)),
                  pl.BlockSpec((1, seq, D_RNN), lambda b: (b, 0, 1)),
                  pl.BlockSpec((CONV_W, D_RNN), lambda b: (0, 0)),
                  vec, mat, vec, mat, vec, vec],
        out_specs=pl.BlockSpec((1, seq, D_RNN), lambda b: (b, 0, 0)),
        out_shape=jax.ShapeDtypeStruct((batch, seq, D_RNN), jnp.float32),
        scratch_shapes=[pltpu.VMEM((RG_CHUNK, D_RNN), jnp.float32)] * 2,
        compiler_params=_params(("parallel",)),
        name="rglru",
    )(po, po, conv_w.astype(jnp.float32), row(conv_b), _block_diag(w_a), row(b_a), _block_diag(w_x), row(b_x), row(lam))


def _router_kernel(x_ref, g_ref, wr_ref, xn_ref, idx_ref, gate_ref):
    x = x_ref[...]
    ms = jnp.mean(x * x, axis=-1, keepdims=True)
    xn = x * lax.rsqrt(ms + NORM_EPS) * g_ref[...]
    xn_ref[...] = xn
    logits = jnp.dot(xn, wr_ref[...], precision=lax.Precision.HIGHEST, preferred_element_type=jnp.float32)
    lane = lax.broadcasted_iota(jnp.int32, logits.shape, 1)
    logits = jnp.where(lane < N_EXPERTS, logits, -jnp.inf)
    m1 = jnp.max(logits, axis=1, keepdims=True)
    i1 = jnp.min(jnp.where(logits == m1, lane, LANES), axis=1, keepdims=True)
    rest = jnp.where(lane == i1, -jnp.inf, logits)
    m2 = jnp.max(rest, axis=1, keepdims=True)
    i2 = jnp.min(jnp.where(rest == m2, lane, LANES), axis=1, keepdims=True)
    e = jnp.exp(m2 - m1)
    g1 = 1.0 / (1.0 + e)
    g2 = e / (1.0 + e)
    idx_ref[...] = jnp.where(lane == 0, i1, jnp.where(lane == 1, i2, 0))
    gate_ref[...] = jnp.where(lane == 0, g1, jnp.where(lane == 1, g2, 0.0))


def _router(x, g, w_router):
    n, d = x.shape
    wr = jnp.pad(w_router.astype(jnp.float32), ((0, 0), (0, LANES - N_EXPERTS)))
    tm = ROW_TILE
    return pl.pallas_call(
        _router_kernel,
        grid=(n // tm,),
        in_specs=[pl.BlockSpec((tm, d), lambda i: (i, 0)),
                  pl.BlockSpec((1, d), lambda i: (0, 0)),
                  pl.BlockSpec((d, LANES), lambda i: (0, 0))],
        out_specs=[pl.BlockSpec((tm, d), lambda i: (i, 0)),
                   pl.BlockSpec((tm, LANES), lambda i: (i, 0)),
                   pl.BlockSpec((tm, LANES), lambda i: (i, 0))],
        out_shape=[jax.ShapeDtypeStruct((n, d), jnp.float32),
                   jax.ShapeDtypeStruct((n, LANES), jnp.int32),
                   jax.ShapeDtypeStruct((n, LANES), jnp.float32)],
        compiler_params=_params(("parallel",)),
        name="moe_router",
    )(x, g.reshape(1, d).astype(jnp.float32), wr)


def _row_copy(src_hbm, dst, src_row, dst_row, sem):
    return pltpu.make_async_copy(src_hbm.at[pl.ds(src_row, 1)], dst.at[pl.ds(dst_row, 1)], sem)


def _gather_kernel(tok_ref, x_hbm, o_ref, sem):
    base = pl.program_id(0) * GATHER_ROWS

    def issue(r, carry):
        _row_copy(x_hbm, o_ref, tok_ref[base + r], r, sem).start()
        return carry

    lax.fori_loop(0, GATHER_ROWS, issue, 0)
    pltpu.make_async_copy(x_hbm.at[pl.ds(0, GATHER_ROWS)], o_ref, sem).wait()


def _gather_rows(row_tok, x):
    n_rows = row_tok.shape[0]
    d = x.shape[1]
    return pl.pallas_call(
        _gather_kernel,
        grid_spec=pltpu.PrefetchScalarGridSpec(
            num_scalar_prefetch=1,
            grid=(n_rows // GATHER_ROWS,),
            in_specs=[pl.BlockSpec(memory_space=pl.ANY)],
            out_specs=pl.BlockSpec((GATHER_ROWS, d), lambda i, tok: (i, 0)),
            scratch_shapes=[pltpu.SemaphoreType.DMA(())]),
        out_shape=jax.ShapeDtypeStruct((n_rows, d), x.dtype),
        compiler_params=_params(("arbitrary",)),
        name="moe_gather",
    )(row_tok, x)


def _experts_kernel(be_ref, nv_ref, x_ref, wg_ref, wu_ref, wd_ref, o_ref, acc_ref):
    i = pl.program_id(0)
    f = pl.program_id(1)
    last = pl.num_programs(1) - 1

    @pl.when(i < nv_ref[0])
    def _():
        @pl.when(f == 0)
        def _():
            acc_ref[...] = jnp.zeros_like(acc_ref)

        xb = _bf16(x_ref[...])
        g = jnp.dot(xb, _bf16(wg_ref[0]), preferred_element_type=jnp.float32)
        u = jnp.dot(xb, _bf16(wu_ref[0]), preferred_element_type=jnp.float32)
        h = (_silu(g) * u).astype(jnp.bfloat16)
        acc_ref[...] += jnp.dot(h, _bf16(wd_ref[0]), preferred_element_type=jnp.float32)

        @pl.when(f == last)
        def _():
            o_ref[...] = acc_ref[...]

    @pl.when((i >= nv_ref[0]) & (f == last))
    def _():
        o_ref[...] = jnp.zeros_like(o_ref)


def _experts(blk_expert, n_valid, x_rows, w_gate, w_up, w_down):
    n_rows, d = x_rows.shape
    ff = w_gate.shape[2]
    tm, tf = MOE_ROW_TILE, FFN_COL_TILE
    n_f = ff // tf

    def fcol(i, f, nv):
        return jnp.where(i < nv[0], f, n_f - 1)

    return pl.pallas_call(
        _experts_kernel,
        grid_spec=pltpu.PrefetchScalarGridSpec(
            num_scalar_prefetch=2,
            grid=(n_rows // tm, n_f),
            in_specs=[pl.BlockSpec((tm, d), lambda i, f, be, nv: (i, 0)),
                      pl.BlockSpec((1, d, tf), lambda i, f, be, nv: (be[i], 0, fcol(i, f, nv))),
                      pl.BlockSpec((1, d, tf), lambda i, f, be, nv: (be[i], 0, fcol(i, f, nv))),
                      pl.BlockSpec((1, tf, d), lambda i, f, be, nv: (be[i], fcol(i, f, nv), 0))],
            out_specs=pl.BlockSpec((tm, d), lambda i, f, be, nv: (i, 0)),
            scratch_shapes=[pltpu.VMEM((tm, d), jnp.float32)]),
        out_shape=jax.ShapeDtypeStruct((n_rows, d), jnp.float32),
        compiler_params=_params(("arbitrary", "arbitrary")),
        name="moe_experts",
    )(blk_expert, n_valid, x_rows, w_gate, w_up, w_down)


def _combine_kernel(dest_ref, y_hbm, x_ref, gate_ref, o_ref, buf, sem):
    base = pl.program_id(0) * GATHER_ROWS * TOP_K

    def issue(r, carry):
        for k in range(TOP_K):
            _row_copy(y_hbm, buf.at[k], dest_ref[base + r * TOP_K + k], r, sem).start()
        return carry

    lax.fori_loop(0, GATHER_ROWS, issue, 0)
    for k in range(TOP_K):
        pltpu.make_async_copy(y_hbm.at[pl.ds(0, GATHER_ROWS)], buf.at[k], sem).wait()
    gate = gate_ref[...]
    o_ref[...] = x_ref[...] + gate[:, 0:1] * buf[0] + gate[:, 1:2] * buf[1]


def _combine(dest, y_rows, x, gates):
    n, d = x.shape
    return pl.pallas_call(
        _combine_kernel,
        grid_spec=pltpu.PrefetchScalarGridSpec(
            num_scalar_prefetch=1,
            grid=(n // GATHER_ROWS,),
            in_specs=[pl.BlockSpec(memory_space=pl.ANY),
                      pl.BlockSpec((GATHER_ROWS, d), lambda i, dest: (i, 0)),
                      pl.BlockSpec((GATHER_ROWS, LANES), lambda i, dest: (i, 0))],
            out_specs=pl.BlockSpec((GATHER_ROWS, d), lambda i, dest: (i, 0)),
            scratch_shapes=[pltpu.VMEM((TOP_K, GATHER_ROWS, d), jnp.float32),
                            pltpu.SemaphoreType.DMA(())]),
        out_shape=jax.ShapeDtypeStruct((n, d), jnp.float32),
        compiler_params=_params(("arbitrary",)),
        name="moe_combine",
    )(dest, y_rows, x, gates)


def _moe(x, g, w_router, w_gate, w_up, w_down):
    n = x.shape[0]
    xn, idx, gates = _router(x, g, w_router)
    e_flat = idx[:, :TOP_K].reshape(-1)
    onehot = (e_flat[:, None] == jnp.arange(N_EXPERTS, dtype=jnp.int32)[None, :]).astype(jnp.int32)
    counts = jnp.sum(onehot, axis=0)
    rank = jnp.sum((jnp.cumsum(onehot, axis=0) - onehot) * onehot, axis=1)
    padded = (counts + MOE_ROW_TILE - 1) // MOE_ROW_TILE * MOE_ROW_TILE
    pad_end = jnp.cumsum(padded)
    pad_start = pad_end - padded
    dest = (pad_start[e_flat] + rank).astype(jnp.int32)
    n_rows = n * TOP_K + N_EXPERTS * MOE_ROW_TILE
    tok_flat = jnp.repeat(jnp.arange(n, dtype=jnp.int32), TOP_K)
    row_tok = jnp.zeros((n_rows,), jnp.int32).at[dest].set(tok_flat)
    n_blk = n_rows // MOE_ROW_TILE
    blk_expert = jnp.clip(jnp.searchsorted(pad_end, jnp.arange(n_blk, dtype=jnp.int32) * MOE_ROW_TILE, side='right'),
                          0, N_EXPERTS - 1).astype(jnp.int32)
    n_valid = (pad_end[-1] // MOE_ROW_TILE).astype(jnp.int32).reshape(1)
    x_rows = _gather_rows(row_tok, xn)
    y_rows = _experts(blk_expert, n_valid, x_rows, w_gate, w_up, w_down)
    return _combine(dest, y_rows, x, gates)


def _pad_cols(w, width):
    return jnp.pad(w, ((0, 0), (0, width - w.shape[1])))


def _even_layer(x, batch, seq, norm_mix, w_in, w_out, norm_ffn, w_gate, w_up, w_down, tabs64, tabs32):
    qa, ka, va, qi, ki, wi, qb, kb, vb = jnp.split(
        w_in, np.cumsum([512, 128, 128, 256, 32, 8, 512, 512])[:].tolist(), axis=1)
    w_rope64 = jnp.concatenate([qa, qb, kb, ka], axis=1)
    w_plain = jnp.concatenate([vb, va, _pad_cols(wi, LANES)], axis=1)
    w_idx = jnp.concatenate([qi, _pad_cols(ki, LANES)], axis=1)
    xn = _rmsnorm(x, norm_mix, name="rmsnorm_mix")
    p64 = _proj(xn, w_rope64, seq=seq, rope=tabs64, rope_half=ROT_DIM // 2, name="proj_rope64")
    pv = _proj(xn, w_plain, seq=seq, name="proj_plain")
    pi = _proj(xn, w_idx, seq=seq, rope=tabs32, rope_half=IDX_ROT // 2, name="proj_idx")
    p64 = p64.reshape(batch, seq, -1)
    pv = pv.reshape(batch, seq, -1)
    pi = pi.reshape(batch, seq, -1)
    out_a = _dsa(p64, pv, pi, batch=batch, seq=seq)
    out_b = _dilated(p64, pv, batch=batch, seq=seq)
    mix = jnp.concatenate([out_a, out_b], axis=-1).reshape(batch * seq, -1)
    x = _proj(mix, w_out, seq=seq, res=x, name="proj_out")
    xn = _rmsnorm(x, norm_ffn, name="rmsnorm_ffn")
    return _ffn(xn, w_gate, w_up, w_down, x)


def _odd_layer(x, batch, seq, norm_mix, w_in, q_norm, w_uq, kv_norm, w_ukv, conv_w, conv_b, w_a, b_a, w_x, b_x, lam,
               w_out, norm_ffn, w_router, w_gate, w_up, w_down, tabs_mla):
    cq, ckv, kr, xr, gr = jnp.split(w_in, np.cumsum([Q_LORA, KV_LORA, C_ROPE, D_RNN]).tolist(), axis=1)
    w_plain = jnp.concatenate([xr, gr, cq, ckv], axis=1)
    xn = _rmsnorm(x, norm_mix, name="rmsnorm_mix")
    po = _proj(xn, w_plain, seq=seq, name="proj_plain")
    k_rope = _proj(xn, _pad_cols(kr, LANES), seq=seq, rope=tabs_mla, rope_half=C_ROPE // 2, name="proj_krope")
    cqn = _rmsnorm(po, q_norm, width=Q_LORA, col_block=2 * D_RNN // Q_LORA, name="rmsnorm_cq")
    ckvn = _rmsnorm(po, kv_norm, width=KV_LORA, col_block=(2 * D_RNN + Q_LORA) // KV_LORA, name="rmsnorm_ckv")
    w_uq = w_uq.reshape(Q_LORA, C_HEADS, C_NOPE + C_ROPE)
    w_qn = w_uq[:, :, :C_NOPE].reshape(Q_LORA, C_HEADS * C_NOPE)
    w_qr = w_uq[:, :, C_NOPE:].reshape(Q_LORA, C_HEADS * C_ROPE)
    w_ukv = w_ukv.reshape(KV_LORA, C_HEADS, C_NOPE + C_V)
    w_kv = jnp.concatenate([w_ukv[:, :, :C_NOPE].reshape(KV_LORA, -1), w_ukv[:, :, C_NOPE:].reshape(KV_LORA, -1)], axis=1)
    qn = _proj(cqn, w_qn, seq=seq, name="proj_qnope")
    qr = _proj(cqn, w_qr, seq=seq, rope=tabs_mla, rope_half=C_ROPE // 2, name="proj_qrope")
    kv = _proj(ckvn, w_kv, seq=seq, name="proj_kv")
    r3 = lambda a: a.reshape(batch, seq, -1)
    out_c = _mla(r3(qn), r3(qr), r3(kv), r3(k_rope), batch=batch, seq=seq)
    out_d = _rglru(r3(po), conv_w, conv_b, w_a, b_a, w_x, b_x, lam, batch=batch, seq=seq)
    mix = jnp.concatenate([out_c, out_d], axis=-1).reshape(batch * seq, -1)
    x = _proj(mix, w_out, seq=seq, res=x, name="proj_out")
    return _moe(x, norm_ffn, w_router, w_gate, w_up, w_down)


def kernel(x, ev_norm_mix, ev_w_in, ev_w_out, ev_norm_ffn, ffn_w_gate, ffn_w_up, ffn_w_down, od_norm_mix, od_w_in, mla_q_norm, mla_w_uq, mla_kv_norm, mla_w_ukv, rg_conv_w, rg_conv_b, rg_w_a, rg_b_a, rg_w_x, rg_b_x, rg_lambda, od_w_out, od_norm_ffn, moe_router, moe_w_gate, moe_w_up, moe_w_down, final_norm):
    batch, seq, d = x.shape
    depth = ev_w_in.shape[0] + od_w_in.shape[0]
    assert d == D_MODEL and seq % KEY_CHUNK == 0 and (batch * seq) % FFN_ROW_TILE == 0
    tabs64 = _rope_tables(seq, HEAD_DIM, ROT_DIM)
    tabs32 = _rope_tables(seq, IDX_DIM, IDX_ROT)
    tabs_mla = _rope_tables(seq, C_ROPE, C_ROPE)
    h = x.reshape(batch * seq, d)
    for layer in range(depth):
        i = layer // 2
        if layer % 2 == 0:
            h = _even_layer(h, batch, seq, ev_norm_mix[i], ev_w_in[i], ev_w_out[i], ev_norm_ffn[i],
                            ffn_w_gate[i], ffn_w_up[i], ffn_w_down[i], tabs64, tabs32)
        else:
            h = _odd_layer(h, batch, seq, od_norm_mix[i], od_w_in[i], mla_q_norm[i], mla_w_uq[i], mla_kv_norm[i],
                           mla_w_ukv[i], rg_conv_w[i], rg_conv_b[i], rg_w_a[i], rg_b_a[i], rg_w_x[i], rg_b_x[i],
                           rg_lambda[i], od_w_out[i], od_norm_ffn[i], moe_router[i], moe_w_gate[i], moe_w_up[i],
                           moe_w_down[i], tabs_mla)
    out = _rmsnorm(h, final_norm, out_dtype=jnp.float32, name="rmsnorm_final")
    return out.reshape(batch, seq, d)
```

```python
import functools

import jax
import jax.numpy as jnp
import numpy as np
from jax import lax
from jax.experimental import pallas as pl
from jax.experimental.pallas import tpu as pltpu

D_MODEL = 1024
HEAD_DIM = 64
ROT_DIM = HEAD_DIM // 4
ROPE_THETA = 500000.0
NORM_EPS = 1e-6

A_HEADS = 8
A_KV_HEADS = 2
IDX_HEADS = 8
IDX_DIM = 32
IDX_ROT = IDX_DIM // 4
TOPK_MAX = 256

B_HEADS = 8
DILATED_PATTERNS = ((128, 1), (512, 4), (2048, 16))

C_HEADS = 8
C_NOPE = 64
C_ROPE = 32
C_V = 64
Q_LORA = 256
KV_LORA = 128

D_RNN = 512
RG_BLOCKS = 8
RG_BW = D_RNN // RG_BLOCKS
CONV_W = 4
RG_C = 8.0

D_FF = 3584
N_EXPERTS = 8
TOP_K = 2

LANES = 128
VMEM_LIMIT_BYTES = 56 * 1024 * 1024
BLOCK_Q = 128
KEY_CHUNK = 512
MLA_BLOCK_Q = 256
LOG2_E = 1.4426950408889634
ROW_TILE = 512
FFN_ROW_TILE = 1024
FFN_COL_TILE = 512
MOE_ROW_TILE = 1024
GATHER_ROWS = 256
RG_CHUNK = 256
NEG_BIG = -1e30
INT_MIN = -2147483648

_NT = (((1,), (1,)), ((), ()))


def _params(semantics, vmem=VMEM_LIMIT_BYTES):
    return pltpu.CompilerParams(dimension_semantics=semantics, vmem_limit_bytes=vmem)


def _bf16(a):
    return a if a.dtype == jnp.bfloat16 else a.astype(jnp.bfloat16)


def _rmsnorm_kernel(x_ref, g_ref, o_ref):
    x = x_ref[...].astype(jnp.float32)
    ms = jnp.mean(x * x, axis=-1, keepdims=True)
    o_ref[...] = (x * lax.rsqrt(ms + NORM_EPS) * g_ref[...]).astype(o_ref.dtype)


def _rmsnorm(x, g, *, width=None, col_block=0, out_dtype=jnp.bfloat16, name="rmsnorm"):
    n = x.shape[0]
    width = x.shape[1] if width is None else width
    return pl.pallas_call(
        _rmsnorm_kernel,
        grid=(n // ROW_TILE,),
        in_specs=[pl.BlockSpec((ROW_TILE, width), lambda i: (i, col_block)),
                  pl.BlockSpec((1, width), lambda i: (0, 0))],
        out_specs=pl.BlockSpec((ROW_TILE, width), lambda i: (i, 0)),
        out_shape=jax.ShapeDtypeStruct((n, width), out_dtype),
        compiler_params=_params(("parallel",)),
        name=name,
    )(x, g.reshape(1, width).astype(jnp.float32))


def _rope_tables(seq, period, rot_dim):
    half = rot_dim // 2
    inv_freq = 1.0 / (ROPE_THETA ** (jnp.arange(half, dtype=jnp.float32) * (2.0 / rot_dim)))
    ang = jnp.arange(seq, dtype=jnp.float32)[:, None] * inv_freq[None, :]
    cos, sin = jnp.cos(ang), jnp.sin(ang)
    rest = period - rot_dim
    ones = jnp.ones((seq, rest), jnp.float32)
    zeros = jnp.zeros((seq, rest), jnp.float32)
    zh = jnp.zeros((seq, half), jnp.float32)
    c = jnp.concatenate([cos, cos, ones], axis=1)
    sa = jnp.concatenate([-sin, zh, zeros], axis=1)
    sb = jnp.concatenate([zh, sin, zeros], axis=1)
    reps = LANES // period
    return tuple(jnp.tile(t, (1, reps)) for t in (c, sa, sb))


def _proj_kernel(*refs, rope_half, has_res):
    a_ref, w_ref = refs[0], refs[1]
    o_ref = refs[-1]
    y = jnp.dot(_bf16(a_ref[...]), _bf16(w_ref[...]), preferred_element_type=jnp.float32)
    pos = 2
    if rope_half:
        c_ref, sa_ref, sb_ref = refs[2:5]
        pos = 5
        tn = y.shape[1]
        reps = tn // LANES
        c = jnp.tile(c_ref[...], (1, reps))
        sa = jnp.tile(sa_ref[...], (1, reps))
        sb = jnp.tile(sb_ref[...], (1, reps))
        y = y * c + pltpu.roll(y, tn - rope_half, 1) * sa + pltpu.roll(y, rope_half, 1) * sb
    if has_res:
        y = y + refs[pos][...]
    o_ref[...] = y.astype(o_ref.dtype)


def _proj(a, w, *, seq, rope=None, rope_half=0, res=None, out_dtype=jnp.float32, name="proj"):
    n, k = a.shape
    np_ = w.shape[1]
    tm = ROW_TILE
    seq_blocks = seq // tm
    in_specs = [pl.BlockSpec((tm, k), lambda i: (i, 0)),
                pl.BlockSpec((k, np_), lambda i: (0, 0))]
    args = [a, w]
    if rope is not None:
        for t in rope:
            in_specs.append(pl.BlockSpec((tm, LANES), lambda i: (i % seq_blocks, 0)))
            args.append(t)
    if res is not None:
        in_specs.append(pl.BlockSpec((tm, np_), lambda i: (i, 0)))
        args.append(res)
    return pl.pallas_call(
        functools.partial(_proj_kernel, rope_half=rope_half if rope is not None else 0, has_res=res is not None),
        grid=(n // tm,),
        in_specs=in_specs,
        out_specs=pl.BlockSpec((tm, np_), lambda i: (i, 0)),
        out_shape=jax.ShapeDtypeStruct((n, np_), out_dtype),
        compiler_params=_params(("parallel",)),
        name=name,
    )(*args)


def _projt_kernel(*refs, rope_half, scale):
    wt_ref, a_ref = refs[0], refs[1]
    o_ref = refs[-1]
    y = lax.dot_general(_bf16(wt_ref[...]), _bf16(a_ref[...]), _NT, preferred_element_type=jnp.float32)
    if rope_half:
        c_ref, sa_ref, sb_ref = refs[2:5]
        rows = y.shape[0]
        reps = rows // LANES
        c = jnp.tile(c_ref[...], (reps, 1))
        sa = jnp.tile(sa_ref[...], (reps, 1))
        sb = jnp.tile(sb_ref[...], (reps, 1))
        y = y * c + pltpu.roll(y, rows - rope_half, 0) * sa + pltpu.roll(y, rope_half, 0) * sb
    if scale != 1.0:
        y = y * scale
    o_ref[0] = y.astype(o_ref.dtype)


def _projt(a, wt, *, seq, rope=None, rope_half=0, scale=1.0, out_dtype=jnp.bfloat16, name="projt"):
    n, k = a.shape
    np_ = wt.shape[0]
    tm = ROW_TILE
    seq_blocks = seq // tm
    in_specs = [pl.BlockSpec((np_, k), lambda i: (0, 0)),
                pl.BlockSpec((tm, k), lambda i: (i, 0))]
    args = [wt, a]
    if rope is not None:
        for t in rope:
            in_specs.append(pl.BlockSpec((LANES, tm), lambda i: (0, i % seq_blocks)))
            args.append(t)
    return pl.pallas_call(
        functools.partial(_projt_kernel, rope_half=rope_half if rope is not None else 0, scale=scale),
        grid=(n // tm,),
        in_specs=in_specs,
        out_specs=pl.BlockSpec((1, np_, tm), lambda i: (i, 0, 0)),
        out_shape=jax.ShapeDtypeStruct((n // tm, np_, tm), out_dtype),
        compiler_params=_params(("parallel",)),
        name=name,
    )(*args)


def _silu(g):
    return g * (1.0 / (1.0 + jnp.exp(-g)))


def _ffn_kernel(xn_ref, wg_ref, wu_ref, wd_ref, res_ref, o_ref, acc_ref):
    f = pl.program_id(1)

    @pl.when(f == 0)
    def _():
        acc_ref[...] = jnp.zeros_like(acc_ref)

    xn = xn_ref[...]
    g = jnp.dot(xn, _bf16(wg_ref[0]), preferred_element_type=jnp.float32)
    u = jnp.dot(xn, _bf16(wu_ref[0]), preferred_element_type=jnp.float32)
    h = (_silu(g) * u).astype(jnp.bfloat16)
    acc_ref[...] += jnp.dot(h, _bf16(wd_ref[0]), preferred_element_type=jnp.float32)

    @pl.when(f == pl.num_programs(1) - 1)
    def _():
        o_ref[...] = res_ref[...] + acc_ref[...]


def _ffn(xn, w_gate, w_up, w_down, res, layer):
    n, d = xn.shape
    ff = w_gate.shape[2]
    tm, tf = FFN_ROW_TILE, FFN_COL_TILE
    return pl.pallas_call(
        _ffn_kernel,
        grid=(n // tm, ff // tf),
        in_specs=[pl.BlockSpec((tm, d), lambda i, f: (i, 0)),
                  pl.BlockSpec((1, d, tf), lambda i, f: (layer, 0, f)),
                  pl.BlockSpec((1, d, tf), lambda i, f: (layer, 0, f)),
                  pl.BlockSpec((1, tf, d), lambda i, f: (layer, f, 0)),
                  pl.BlockSpec((tm, d), lambda i, f: (i, 0))],
        out_specs=pl.BlockSpec((tm, d), lambda i, f: (i, 0)),
        out_shape=jax.ShapeDtypeStruct((n, d), jnp.float32),
        scratch_shapes=[pltpu.VMEM((tm, d), jnp.float32)],
        compiler_params=_params(("parallel", "arbitrary")),
        name="ffn_swiglu",
    )(xn, w_gate, w_up, w_down, res)


def _dsa_kernel(q_ref, k_ref, v_ref, qi_ref, ki_ref, wi_ref, o_ref, key_scr, bias_scr, *, topk):
    i = pl.program_id(1)
    n_chunks = (i * BLOCK_Q + BLOCK_Q + KEY_CHUNK - 1) // KEY_CHUNK
    qpos = i * BLOCK_Q + lax.broadcasted_iota(jnp.int32, (BLOCK_Q, 1), 0)
    kloc = lax.broadcasted_iota(jnp.int32, (1, KEY_CHUNK), 1)

    qi = _bf16(qi_ref[0])
    w = wi_ref[0][:, :IDX_HEADS] * (IDX_DIM ** -0.5 * IDX_HEADS ** -0.5)

    def index_body(c, carry):
        kc = _bf16(ki_ref[0, pl.ds(c * KEY_CHUNK, KEY_CHUNK), :])[:, :IDX_DIM]
        score = jnp.zeros((BLOCK_Q, KEY_CHUNK), jnp.float32)
        for h in range(IDX_HEADS):
            lg = lax.dot_general(qi[:, h * IDX_DIM:(h + 1) * IDX_DIM], kc, _NT,
                                 preferred_element_type=jnp.float32)
            score = score + w[:, h:h + 1] * jnp.maximum(lg, 0.0)
        bits = pltpu.bitcast(score, jnp.int32)
        key = jnp.where(bits < 0, bits ^ jnp.int32(0x7FFFFFFF), bits)
        key_scr[c] = jnp.where(c * KEY_CHUNK + kloc <= qpos, key, jnp.int32(INT_MIN))
        return carry

    lax.fori_loop(0, n_chunks, index_body, 0)

    def count(pred):
        def body(c, acc):
            return acc + pred(key_scr[c]).astype(jnp.int32)
        acc = lax.fori_loop(0, n_chunks, body, jnp.zeros((BLOCK_Q, KEY_CHUNK), jnp.int32))
        return jnp.sum(acc, axis=1, keepdims=True)

    def bit_body(t, res):
        cand = res | jnp.left_shift(jnp.int32(1), 31 - t)
        cand_key = cand ^ jnp.int32(INT_MIN)
        return jnp.where(count(lambda key: key >= cand_key) >= topk, cand, res)

    thr = lax.fori_loop(0, 32, bit_body, jnp.zeros((BLOCK_Q, 1), jnp.int32)) ^ jnp.int32(INT_MIN)
    need = (topk - count(lambda key: key > thr)).astype(jnp.float32)

    tri = (lax.broadcasted_iota(jnp.int32, (KEY_CHUNK, KEY_CHUNK), 0)
           <= lax.broadcasted_iota(jnp.int32, (KEY_CHUNK, KEY_CHUNK), 1)).astype(jnp.bfloat16)

    def mask_body(c, seen):
        key = key_scr[c]
        eq = key == thr
        running = jnp.dot(eq.astype(jnp.bfloat16), tri, preferred_element_type=jnp.float32) + seen
        sel = ((key > thr) | (eq & (running <= need))) & (c * KEY_CHUNK + kloc <= qpos)
        bias_scr[c] = jnp.where(sel, 0.0, NEG_BIG)
        return running[:, KEY_CHUNK - 1:KEY_CHUNK]

    lax.fori_loop(0, n_chunks, mask_body, jnp.zeros((BLOCK_Q, 1), jnp.float32))

    q = _bf16(q_ref[0])
    scale = HEAD_DIM ** -0.5
    group = A_HEADS // A_KV_HEADS
    for n in range(A_KV_HEADS):
        def att_body(c, carry, n=n):
            rows = pl.ds(c * KEY_CHUNK, KEY_CHUNK)
            kc = _bf16(k_ref[0, rows, :])[:, n * HEAD_DIM:(n + 1) * HEAD_DIM]
            vc = _bf16(v_ref[0, rows, :])[:, n * HEAD_DIM:(n + 1) * HEAD_DIM]
            bias = bias_scr[c]
            out = []
            for g in range(group):
                m, l, acc = carry[g]
                h = n * group + g
                s = lax.dot_general(q[:, h * HEAD_DIM:(h + 1) * HEAD_DIM], kc, _NT,
                                    preferred_element_type=jnp.float32) * scale + bias
                m_new = jnp.maximum(m, jnp.max(s, axis=1, keepdims=True))
                alpha = jnp.exp(m - m_new)
                p = jnp.exp(s - m_new)
                l = l * alpha + jnp.sum(p, axis=1, keepdims=True)
                acc = acc * alpha + jnp.dot(p.astype(jnp.bfloat16), vc, preferred_element_type=jnp.float32)
                out.append((m_new, l, acc))
            return tuple(out)

        init = tuple((jnp.full((BLOCK_Q, 1), NEG_BIG, jnp.float32),
                      jnp.zeros((BLOCK_Q, 1), jnp.float32),
                      jnp.zeros((BLOCK_Q, HEAD_DIM), jnp.float32)) for _ in range(group))
        state = lax.fori_loop(0, n_chunks, att_body, init)
        for g in range(group):
            h = n * group + g
            _, l, acc = state[g]
            o_ref[0, :, h * HEAD_DIM:(h + 1) * HEAD_DIM] = acc * (1.0 / l)


def _dsa(p64, pv, pi, *, batch, seq):
    topk = min(TOPK_MAX, seq // 4)
    n_chunks = seq // KEY_CHUNK
    qw = A_HEADS * HEAD_DIM
    return pl.pallas_call(
        functools.partial(_dsa_kernel, topk=topk),
        grid=(batch, seq // BLOCK_Q),
        in_specs=[pl.BlockSpec((1, BLOCK_Q, qw), lambda b, i: (b, i, 0)),
                  pl.BlockSpec((1, seq, LANES), lambda b, i: (b, 0, 12)),
                  pl.BlockSpec((1, seq, LANES), lambda b, i: (b, 0, 4)),
                  pl.BlockSpec((1, BLOCK_Q, IDX_HEADS * IDX_DIM), lambda b, i: (b, i, 0)),
                  pl.BlockSpec((1, seq, LANES), lambda b, i: (b, 0, 2)),
                  pl.BlockSpec((1, BLOCK_Q, LANES), lambda b, i: (b, i, 5))],
        out_specs=pl.BlockSpec((1, BLOCK_Q, qw), lambda b, i: (b, i, 0)),
        out_shape=jax.ShapeDtypeStruct((batch, seq, qw), jnp.float32),
        scratch_shapes=[pltpu.VMEM((n_chunks, BLOCK_Q, KEY_CHUNK), jnp.int32),
                        pltpu.VMEM((n_chunks, BLOCK_Q, KEY_CHUNK), jnp.float32)],
        compiler_params=_params(("parallel", "arbitrary")),
        name="dsa_attention",
    )(p64, p64, pv, pi, pi, pv)


def _dilated_kernel(q_ref, k_ref, v_ref, o_ref, acc_scr, m_scr, l_scr, *, seq):
    scale = HEAD_DIM ** -0.5
    row = lax.broadcasted_iota(jnp.int32, (BLOCK_Q, BLOCK_Q), 0)
    col = lax.broadcasted_iota(jnp.int32, (BLOCK_Q, BLOCK_Q), 1)
    mask_cur = col <= row
    mask_prev = col >= row

    acc_scr[...] = jnp.zeros_like(acc_scr)
    l_scr[...] = jnp.zeros_like(l_scr)
    m_scr[...] = jnp.full_like(m_scr, NEG_BIG)

    def spread(cols):
        return jnp.concatenate([jnp.broadcast_to(c, (BLOCK_Q, HEAD_DIM)) for c in cols], axis=1)

    for (window, dil) in DILATED_PATTERNS:
        assert window // dil == BLOCK_Q and seq % (dil * BLOCK_Q) == 0
        n_blocks = seq // (dil * BLOCK_Q)

        def block_body(idx, carry, dil=dil, n_blocks=n_blocks):
            phase = idx // n_blocks
            blk = idx % n_blocks
            rows = pl.ds(phase + dil * BLOCK_Q * blk, BLOCK_Q, stride=dil)
            q2 = _bf16(q_ref[0, rows, :])
            k_cur = _bf16(k_ref[0, rows, :])
            v_cur = _bf16(v_ref[0, rows, :])
            if n_blocks > 1:
                rows_p = pl.ds(phase + dil * BLOCK_Q * jnp.maximum(blk - 1, 0), BLOCK_Q, stride=dil)
                k_prev = _bf16(k_ref[0, rows_p, :])
                v_prev = _bf16(v_ref[0, rows_p, :])
                mask_p = mask_prev & (blk > 0)
            m_old = m_scr[rows, :]
            new_m, new_sum, new_pv = [], [], []
            for h in range(2):
                sl = slice(h * HEAD_DIM, (h + 1) * HEAD_DIM)
                s_c = lax.dot_general(q2[:, sl], k_cur[:, sl], _NT, preferred_element_type=jnp.float32) * scale
                s_c = jnp.where(mask_cur, s_c, NEG_BIG)
                mx = jnp.max(s_c, axis=1, keepdims=True)
                if n_blocks > 1:
                    s_p = lax.dot_general(q2[:, sl], k_prev[:, sl], _NT, preferred_element_type=jnp.float32) * scale
                    s_p = jnp.where(mask_p, s_p, NEG_BIG)
                    mx = jnp.maximum(mx, jnp.max(s_p, axis=1, keepdims=True))
                m_h = jnp.maximum(m_old[:, h * HEAD_DIM:h * HEAD_DIM + 1], mx)
                p_c = jnp.exp(s_c - m_h)
                psum = jnp.sum(p_c, axis=1, keepdims=True)
                pv = jnp.dot(p_c.astype(jnp.bfloat16), v_cur[:, sl], preferred_element_type=jnp.float32)
                if n_blocks > 1:
                    p_p = jnp.exp(s_p - m_h)
                    psum = psum + jnp.sum(p_p, axis=1, keepdims=True)
                    pv = pv + jnp.dot(p_p.astype(jnp.bfloat16), v_prev[:, sl], preferred_element_type=jnp.float32)
                new_m.append(m_h)
                new_sum.append(psum)
                new_pv.append(pv)
            m_new = spread(new_m)
            alpha = jnp.exp(m_old - m_new)
            m_scr[rows, :] = m_new
            l_scr[rows, :] = l_scr[rows, :] * alpha + spread(new_sum)
            acc_scr[rows, :] = acc_scr[rows, :] * alpha + jnp.concatenate(new_pv, axis=1)
            return carry

        lax.fori_loop(0, dil * n_blocks, block_body, 0)

    o_ref[0] = acc_scr[...] * (1.0 / l_scr[...])


def _dilated(p64, pv, *, batch, seq):
    pairs = B_HEADS // 2
    return pl.pallas_call(
        functools.partial(_dilated_kernel, seq=seq),
        grid=(batch, pairs),
        in_specs=[pl.BlockSpec((1, seq, LANES), lambda b, p: (b, 0, 4 + p)),
                  pl.BlockSpec((1, seq, LANES), lambda b, p: (b, 0, 8 + p)),
                  pl.BlockSpec((1, seq, LANES), lambda b, p: (b, 0, p))],
        out_specs=pl.BlockSpec((1, seq, LANES), lambda b, p: (b, 0, p)),
        out_shape=jax.ShapeDtypeStruct((batch, seq, B_HEADS * HEAD_DIM), jnp.float32),
        scratch_shapes=[pltpu.VMEM((seq, LANES), jnp.float32)] * 3,
        compiler_params=_params(("parallel", "parallel")),
        name="dilated_attention",
    )(p64, p64, pv)


def _mla_kernel(qn_ref, qr_ref, kn_ref, kr_ref, vt_ref, o_ref, *scr):
    acc_scr, m_scr, l_scr = scr[:C_HEADS], scr[C_HEADS:2 * C_HEADS], scr[2 * C_HEADS:]
    i = pl.program_id(1)
    tq = MLA_BLOCK_Q
    n_full = (i * tq + 1) // KEY_CHUNK
    n_all = (i * tq + tq + KEY_CHUNK - 1) // KEY_CHUNK
    qpos = i * tq + lax.broadcasted_iota(jnp.int32, (1, tq), 1)
    kloc = lax.broadcasted_iota(jnp.int32, (KEY_CHUNK, 1), 0)

    for h in range(C_HEADS):
        acc_scr[h][...] = jnp.zeros_like(acc_scr[h])
        l_scr[h][...] = jnp.zeros_like(l_scr[h])
        m_scr[h][...] = jnp.full_like(m_scr[h], NEG_BIG)

    def chunk(c, carry, masked):
        rows = pl.ds(c * KEY_CHUNK, KEY_CHUNK)
        kr = kr_ref[rows, :][:, :C_ROPE]
        visible = (c * KEY_CHUNK + kloc) <= qpos
        scores = []
        for h in range(C_HEADS):
            hn = slice(h * C_NOPE, (h + 1) * C_NOPE)
            scores.append(jnp.dot(kn_ref[rows, hn], qn_ref[0, hn, :], preferred_element_type=jnp.float32)
                          + jnp.dot(kr, qr_ref[0, h * C_ROPE:(h + 1) * C_ROPE, :],
                                    preferred_element_type=jnp.float32))
        probs, alphas = [], []
        for h in range(C_HEADS):
            s = jnp.where(visible, scores[h], NEG_BIG) if masked else scores[h]
            m_old = m_scr[h][...]
            m_new = jnp.maximum(m_old, jnp.max(s, axis=0, keepdims=True))
            alpha = jnp.exp2(m_old - m_new)
            p = jnp.exp2(s - m_new)
            l_scr[h][...] = l_scr[h][...] * alpha + jnp.sum(p, axis=0, keepdims=True)
            m_scr[h][...] = m_new
            probs.append(p.astype(jnp.bfloat16))
            alphas.append(alpha)
        for h in range(C_HEADS):
            hv = slice(h * C_V, (h + 1) * C_V)
            acc_scr[h][...] = acc_scr[h][...] * alphas[h] + jnp.dot(vt_ref[c, hv, :], probs[h],
                                                                     preferred_element_type=jnp.float32)
        return carry

    lax.fori_loop(0, n_full, functools.partial(chunk, masked=False), 0)
    lax.fori_loop(n_full, n_all, functools.partial(chunk, masked=True), 0)

    out_t = jnp.concatenate([acc_scr[h][...] * (1.0 / l_scr[h][...]) for h in range(C_HEADS)], axis=0)
    o_ref[...] = out_t.T


def _mla(qn_t, qr_t, kn, kr, v_t, *, batch, seq):
    tq = MLA_BLOCK_Q
    tiles = seq // ROW_TILE
    per_tile = ROW_TILE // tq
    kw = C_HEADS * C_NOPE
    vw = C_HEADS * C_V
    n = batch * seq
    return pl.pallas_call(
        _mla_kernel,
        grid=(batch, seq // tq),
        in_specs=[pl.BlockSpec((1, kw, tq), lambda b, i: (b * tiles + i // per_tile, 0, i % per_tile)),
                  pl.BlockSpec((1, C_HEADS * C_ROPE, tq), lambda b, i: (b * tiles + i // per_tile, 0, i % per_tile)),
                  pl.BlockSpec((seq, kw), lambda b, i: (b, 0)),
                  pl.BlockSpec((seq, LANES), lambda b, i: (b, 0)),
                  pl.BlockSpec((tiles, vw, ROW_TILE), lambda b, i: (b, 0, 0))],
        out_specs=pl.BlockSpec((tq, vw), lambda b, i: (b * (seq // tq) + i, 0)),
        out_shape=jax.ShapeDtypeStruct((n, vw), jnp.float32),
        scratch_shapes=([pltpu.VMEM((C_V, tq), jnp.float32)] * C_HEADS
                        + [pltpu.VMEM((1, tq), jnp.float32)] * (2 * C_HEADS)),
        compiler_params=_params(("parallel", "arbitrary")),
        name="mla_attention",
    )(qn_t, qr_t, kn, kr, v_t)


def _expm1(y):
    u = jnp.exp(y)
    safe = jnp.where(u == 1.0, 2.0, u)
    return jnp.where(u == 1.0, y, (u - 1.0) * y / jnp.log(safe))


def _gelu_tanh(x):
    return 0.5 * x * (1.0 + jnp.tanh(np.sqrt(2.0 / np.pi).astype(np.float32) * (x + 0.044715 * (x * x * x))))


def _rglru_kernel(x_ref, g_ref, cw_ref, cb_ref, wa_ref, ba_ref, wx_ref, bx_ref, lam_ref, o_ref, a_scr, u_scr, *, seq):
    cw = cw_ref[...]
    neg_c_softplus = -RG_C * jnp.log1p(jnp.exp(-jnp.abs(-lam_ref[...]))) - RG_C * jnp.maximum(-lam_ref[...], 0.0)
    hi = lax.Precision.HIGHEST
    sub = 8

    def chunk_body(ci, h):
        r0 = pl.multiple_of(ci * RG_CHUNK, RG_CHUNK)
        xa = x_ref[0, pl.ds(r0, RG_CHUNK), :]
        prev = x_ref[0, pl.ds(pl.multiple_of(jnp.maximum(r0 - sub, 0), sub), sub), :]
        prev = jnp.where(ci > 0, prev, 0.0)
        xcat = jnp.concatenate([prev, xa], axis=0)
        xc = cw[CONV_W - 1:CONV_W, :] * xa + cb_ref[...]
        for j in range(1, CONV_W):
            xc = xc + cw[CONV_W - 1 - j:CONV_W - j, :] * pltpu.roll(xcat, j, 0)[sub:, :]
        r = 1.0 / (1.0 + jnp.exp(-(jnp.dot(xc, wa_ref[...], precision=hi, preferred_element_type=jnp.float32)
                                   + ba_ref[...])))
        ig = 1.0 / (1.0 + jnp.exp(-(jnp.dot(xc, wx_ref[...], precision=hi, preferred_element_type=jnp.float32)
                                    + bx_ref[...])))
        log_a = neg_c_softplus * r
        a_scr[...] = jnp.exp(log_a)
        u_scr[...] = jnp.sqrt(-_expm1(2.0 * log_a)) * (ig * xc)
        gate = _gelu_tanh(g_ref[0, pl.ds(r0, RG_CHUNK), :])

        def group_body(gi, h):
            g0 = pl.multiple_of(gi * sub, sub)
            a8 = a_scr[pl.ds(g0, sub), :]
            u8 = u_scr[pl.ds(g0, sub), :]
            rows = []
            for t in range(sub):
                h = a8[t:t + 1, :] * h + u8[t:t + 1, :]
                rows.append(h)
            u_scr[pl.ds(g0, sub), :] = jnp.concatenate(rows, axis=0)
            return h

        h = lax.fori_loop(0, RG_CHUNK // sub, group_body, h)
        o_ref[0, pl.ds(r0, RG_CHUNK), :] = u_scr[...] * gate
        return h

    lax.fori_loop(0, seq // RG_CHUNK, chunk_body, jnp.zeros((1, D_RNN), jnp.float32))


def _block_diag(w):
    nb, bw, _ = w.shape
    eye = jnp.eye(nb, dtype=w.dtype)
    return (eye[:, None, :, None] * w[:, :, None, :]).reshape(nb * bw, nb * bw)


def _rglru(po, conv_w, conv_b, w_a, b_a, w_x, b_x, lam, *, batch, seq):
    row = lambda a: a.reshape(1, D_RNN).astype(jnp.float32)
    vec = pl.BlockSpec((1, D_RNN), lambda b: (0, 0))
    mat = pl.BlockSpec((D_RNN, D_RNN), lambda b: (0, 0))
    return pl.pallas_call(
        functools.partial(_rglru_kernel, seq=seq),
        grid=(batch,),
        in_specs=[pl.BlockSpec((1, seq, D_RNN), lambda b: (b, 0, 0)),
                  pl.BlockSpec((1, seq, D_RNN), lambda b: (b, 0, 1)),
                  pl.BlockSpec((CONV_W, D_RNN), lambda b: (0, 0)),
                  vec, mat, vec, mat, vec, vec],
        out_specs=pl.BlockSpec((1, seq, D_RNN), lambda b: (b, 0, 0)),
        out_shape=jax.ShapeDtypeStruct((batch, seq, D_RNN), jnp.float32),
        scratch_shapes=[pltpu.VMEM((RG_CHUNK, D_RNN), jnp.float32)] * 2,
        compiler_params=_params(("parallel",)),
        name="rglru",
    )(po, po, conv_w.astype(jnp.float32), row(conv_b), _block_diag(w_a), row(b_a), _block_diag(w_x), row(b_x), row(lam))


def _router_kernel(x_ref, g_ref, wr_ref, xn_ref, idx_ref, gate_ref):
    x = x_ref[...]
    ms = jnp.mean(x * x, axis=-1, keepdims=True)
    xn = x * lax.rsqrt(ms + NORM_EPS) * g_ref[...]
    xn_ref[...] = xn
    logits = jnp.dot(xn, wr_ref[...], precision=lax.Precision.HIGHEST, preferred_element_type=jnp.float32)
    lane = lax.broadcasted_iota(jnp.int32, logits.shape, 1)
    logits = jnp.where(lane < N_EXPERTS, logits, -jnp.inf)
    m1 = jnp.max(logits, axis=1, keepdims=True)
    i1 = jnp.min(jnp.where(logits == m1, lane, LANES), axis=1, keepdims=True)
    rest = jnp.where(lane == i1, -jnp.inf, logits)
    m2 = jnp.max(rest, axis=1, keepdims=True)
    i2 = jnp.min(jnp.where(rest == m2, lane, LANES), axis=1, keepdims=True)
    e = jnp.exp(m2 - m1)
    g1 = 1.0 / (1.0 + e)
    g2 = e / (1.0 + e)
    idx_ref[...] = jnp.where(lane == 0, i1, jnp.where(lane == 1, i2, 0))
    gate_ref[...] = jnp.where(lane == 0, g1, jnp.where(lane == 1, g2, 0.0))


def _router(x, g, w_router):
    n, d = x.shape
    wr = jnp.pad(w_router.astype(jnp.float32), ((0, 0), (0, LANES - N_EXPERTS)))
    tm = ROW_TILE
    return pl.pallas_call(
        _router_kernel,
        grid=(n // tm,),
        in_specs=[pl.BlockSpec((tm, d), lambda i: (i, 0)),
                  pl.BlockSpec((1, d), lambda i: (0, 0)),
                  pl.BlockSpec((d, LANES), lambda i: (0, 0))],
        out_specs=[pl.BlockSpec((tm, d), lambda i: (i, 0)),
                   pl.BlockSpec((tm, LANES), lambda i: (i, 0)),
                   pl.BlockSpec((tm, LANES), lambda i: (i, 0))],
        out_shape=[jax.ShapeDtypeStruct((n, d), jnp.float32),
                   jax.ShapeDtypeStruct((n, LANES), jnp.int32),
                   jax.ShapeDtypeStruct((n, LANES), jnp.float32)],
        compiler_params=_params(("parallel",)),
        name="moe_router",
    )(x, g.reshape(1, d).astype(jnp.float32), wr)


def _row_copy(src_hbm, dst, src_row, dst_row, sem):
    return pltpu.make_async_copy(src_hbm.at[pl.ds(src_row, 1)], dst.at[pl.ds(dst_row, 1)], sem)


def _gather_kernel(tok_ref, x_hbm, o_ref, sem):
    base = pl.program_id(0) * GATHER_ROWS

    def issue(r, carry):
        _row_copy(x_hbm, o_ref, tok_ref[base + r], r, sem).start()
        return carry

    lax.fori_loop(0, GATHER_ROWS, issue, 0)
    pltpu.make_async_copy(x_hbm.at[pl.ds(0, GATHER_ROWS)], o_ref, sem).wait()


def _gather_rows(row_tok, x):
    n_rows = row_tok.shape[0]
    d = x.shape[1]
    return pl.pallas_call(
        _gather_kernel,
        grid_spec=pltpu.PrefetchScalarGridSpec(
            num_scalar_prefetch=1,
            grid=(n_rows // GATHER_ROWS,),
            in_specs=[pl.BlockSpec(memory_space=pl.ANY)],
            out_specs=pl.BlockSpec((GATHER_ROWS, d), lambda i, tok: (i, 0)),
            scratch_shapes=[pltpu.SemaphoreType.DMA(())]),
        out_shape=jax.ShapeDtypeStruct((n_rows, d), x.dtype),
        compiler_params=_params(("arbitrary",)),
        name="moe_gather",
    )(row_tok, x)


def _experts_kernel(be_ref, nv_ref, x_ref, wg_ref, wu_ref, wd_ref, o_ref, acc_ref):
    i = pl.program_id(0)
    f = pl.program_id(1)
    last = pl.num_programs(1) - 1

    @pl.when(i < nv_ref[0])
    def _():
        @pl.when(f == 0)
        def _():
            acc_ref[...] = jnp.zeros_like(acc_ref)

        xb = _bf16(x_ref[...])
        g = jnp.dot(xb, _bf16(wg_ref[0]), preferred_element_type=jnp.float32)
        u = jnp.dot(xb, _bf16(wu_ref[0]), preferred_element_type=jnp.float32)
        h = (_silu(g) * u).astype(jnp.bfloat16)
        acc_ref[...] += jnp.dot(h, _bf16(wd_ref[0]), preferred_element_type=jnp.float32)

        @pl.when(f == last)
        def _():
            o_ref[...] = acc_ref[...]

    @pl.when((i >= nv_ref[0]) & (f == last))
    def _():
        o_ref[...] = jnp.zeros_like(o_ref)


def _experts(blk_expert, n_valid, x_rows, w_gate, w_up, w_down):
    n_rows, d = x_rows.shape
    ff = w_gate.shape[2]
    tm, tf = MOE_ROW_TILE, FFN_COL_TILE
    n_f = ff // tf

    def fcol(i, f, nv):
        return jnp.where(i < nv[0], f, n_f - 1)

    return pl.pallas_call(
        _experts_kernel,
        grid_spec=pltpu.PrefetchScalarGridSpec(
            num_scalar_prefetch=2,
            grid=(n_rows // tm, n_f),
            in_specs=[pl.BlockSpec((tm, d), lambda i, f, be, nv: (i, 0)),
                      pl.BlockSpec((1, d, tf), lambda i, f, be, nv: (be[i], 0, fcol(i, f, nv))),
                      pl.BlockSpec((1, d, tf), lambda i, f, be, nv: (be[i], 0, fcol(i, f, nv))),
                      pl.BlockSpec((1, tf, d), lambda i, f, be, nv: (be[i], fcol(i, f, nv), 0))],
            out_specs=pl.BlockSpec((tm, d), lambda i, f, be, nv: (i, 0)),
            scratch_shapes=[pltpu.VMEM((tm, d), jnp.float32)]),
        out_shape=jax.ShapeDtypeStruct((n_rows, d), jnp.float32),
        compiler_params=_params(("arbitrary", "arbitrary")),
        name="moe_experts",
    )(blk_expert, n_valid, x_rows, w_gate, w_up, w_down)


def _combine_kernel(dest_ref, y_hbm, x_ref, gate_ref, o_ref, buf, sem):
    base = pl.program_id(0) * GATHER_ROWS * TOP_K

    def issue(r, carry):
        for k in range(TOP_K):
            _row_copy(y_hbm, buf.at[k], dest_ref[base + r * TOP_K + k], r, sem).start()
        return carry

    lax.fori_loop(0, GATHER_ROWS, issue, 0)
    for k in range(TOP_K):
        pltpu.make_async_copy(y_hbm.at[pl.ds(0, GATHER_ROWS)], buf.at[k], sem).wait()
    gate = gate_ref[...]
    o_ref[...] = x_ref[...] + gate[:, 0:1] * buf[0] + gate[:, 1:2] * buf[1]


def _combine(dest, y_rows, x, gates):
    n, d = x.shape
    return pl.pallas_call(
        _combine_kernel,
        grid_spec=pltpu.PrefetchScalarGridSpec(
            num_scalar_prefetch=1,
            grid=(n // GATHER_ROWS,),
            in_specs=[pl.BlockSpec(memory_space=pl.ANY),
                      pl.BlockSpec((GATHER_ROWS, d), lambda i, dest: (i, 0)),
                      pl.BlockSpec((GATHER_ROWS, LANES), lambda i, dest: (i, 0))],
            out_specs=pl.BlockSpec((GATHER_ROWS, d), lambda i, dest: (i, 0)),
            scratch_shapes=[pltpu.VMEM((TOP_K, GATHER_ROWS, d), jnp.float32),
                            pltpu.SemaphoreType.DMA(())]),
        out_shape=jax.ShapeDtypeStruct((n, d), jnp.float32),
        compiler_params=_params(("arbitrary",)),
        name="moe_combine",
    )(dest, y_rows, x, gates)


def _moe(x, g, w_router, w_gate, w_up, w_down, layer):
    n = x.shape[0]
    w_gate, w_up, w_down = (w.reshape((-1,) + w.shape[2:]) for w in (w_gate, w_up, w_down))
    xn, idx, gates = _router(x, g, w_router)
    e_flat = idx[:, :TOP_K].reshape(-1)
    onehot = (e_flat[:, None] == jnp.arange(N_EXPERTS, dtype=jnp.int32)[None, :]).astype(jnp.int32)
    counts = jnp.sum(onehot, axis=0)
    rank = jnp.sum((jnp.cumsum(onehot, axis=0) - onehot) * onehot, axis=1)
    padded = (counts + MOE_ROW_TILE - 1) // MOE_ROW_TILE * MOE_ROW_TILE
    pad_end = jnp.cumsum(padded)
    pad_start = pad_end - padded
    dest = (pad_start[e_flat] + rank).astype(jnp.int32)
    n_rows = n * TOP_K + N_EXPERTS * MOE_ROW_TILE
    tok_flat = jnp.repeat(jnp.arange(n, dtype=jnp.int32), TOP_K)
    row_tok = jnp.zeros((n_rows,), jnp.int32).at[dest].set(tok_flat)
    n_blk = n_rows // MOE_ROW_TILE
    blk_expert = jnp.clip(jnp.searchsorted(pad_end, jnp.arange(n_blk, dtype=jnp.int32) * MOE_ROW_TILE, side='right'),
                          0, N_EXPERTS - 1).astype(jnp.int32)
    n_valid = (pad_end[-1] // MOE_ROW_TILE).astype(jnp.int32).reshape(1)
    x_rows = _gather_rows(row_tok, xn)
    y_rows = _experts(blk_expert + layer * N_EXPERTS, n_valid, x_rows, w_gate, w_up, w_down)
    return _combine(dest, y_rows, x, gates)


def _pad_cols(w, width):
    return jnp.pad(w, ((0, 0), (0, width - w.shape[1])))


def _even_layer(x, batch, seq, layer, norm_mix, w_in, w_out, norm_ffn, w_gate, w_up, w_down, tabs64, tabs32):
    qa, ka, va, qi, ki, wi, qb, kb, vb = jnp.split(
        w_in, np.cumsum([512, 128, 128, 256, 32, 8, 512, 512])[:].tolist(), axis=1)
    w_rope64 = jnp.concatenate([qa, qb, kb, ka], axis=1)
    w_plain = jnp.concatenate([vb, va, _pad_cols(wi, LANES)], axis=1)
    w_idx = jnp.concatenate([qi, _pad_cols(ki, LANES)], axis=1)
    xn = _rmsnorm(x, norm_mix, name="rmsnorm_mix")
    p64 = _proj(xn, w_rope64, seq=seq, rope=tabs64, rope_half=ROT_DIM // 2, name="proj_rope64")
    pv = _proj(xn, w_plain, seq=seq, name="proj_plain")
    pi = _proj(xn, w_idx, seq=seq, rope=tabs32, rope_half=IDX_ROT // 2, name="proj_idx")
    p64 = p64.reshape(batch, seq, -1)
    pv = pv.reshape(batch, seq, -1)
    pi = pi.reshape(batch, seq, -1)
    out_a = _dsa(p64, pv, pi, batch=batch, seq=seq)
    out_b = _dilated(p64, pv, batch=batch, seq=seq)
    mix = jnp.concatenate([out_a, out_b], axis=-1).reshape(batch * seq, -1)
    x = _proj(mix, w_out, seq=seq, res=x, name="proj_out")
    xn = _rmsnorm(x, norm_ffn, name="rmsnorm_ffn")
    return _ffn(xn, w_gate, w_up, w_down, x, layer)


def _odd_layer(x, batch, seq, layer, norm_mix, w_in, q_norm, w_uq, kv_norm, w_ukv, conv_w, conv_b, w_a, b_a, w_x, b_x, lam,
               w_out, norm_ffn, w_router, w_gate, w_up, w_down, tabs_mla):
    cq, ckv, kr, xr, gr = jnp.split(w_in, np.cumsum([Q_LORA, KV_LORA, C_ROPE, D_RNN]).tolist(), axis=1)
    w_plain = jnp.concatenate([xr, gr, cq, ckv], axis=1)
    xn = _rmsnorm(x, norm_mix, name="rmsnorm_mix")
    po = _proj(xn, w_plain, seq=seq, name="proj_plain")
    k_rope = _proj(xn, _pad_cols(kr, LANES), seq=seq, rope=tabs_mla, rope_half=C_ROPE // 2,
                   out_dtype=jnp.bfloat16, name="proj_krope")
    cqn = _rmsnorm(po, q_norm, width=Q_LORA, col_block=2 * D_RNN // Q_LORA, name="rmsnorm_cq")
    ckvn = _rmsnorm(po, kv_norm, width=KV_LORA, col_block=(2 * D_RNN + Q_LORA) // KV_LORA, name="rmsnorm_ckv")
    w_uq = w_uq.reshape(Q_LORA, C_HEADS, C_NOPE + C_ROPE)
    w_qn_t = w_uq[:, :, :C_NOPE].reshape(Q_LORA, C_HEADS * C_NOPE).T
    w_qr_t = w_uq[:, :, C_NOPE:].reshape(Q_LORA, C_HEADS * C_ROPE).T
    w_ukv = w_ukv.reshape(KV_LORA, C_HEADS, C_NOPE + C_V)
    w_kn = w_ukv[:, :, :C_NOPE].reshape(KV_LORA, C_HEADS * C_NOPE)
    w_v_t = w_ukv[:, :, C_NOPE:].reshape(KV_LORA, C_HEADS * C_V).T
    q_scale = (C_NOPE + C_ROPE) ** -0.5 * LOG2_E
    tabs_mla_t = tuple(t.T for t in tabs_mla)
    qn_t = _projt(cqn, w_qn_t, seq=seq, scale=q_scale, name="projt_qnope")
    qr_t = _projt(cqn, w_qr_t, seq=seq, rope=tabs_mla_t, rope_half=C_ROPE // 2, scale=q_scale, name="projt_qrope")
    kn = _proj(ckvn, w_kn, seq=seq, out_dtype=jnp.bfloat16, name="proj_knope")
    v_t = _projt(ckvn, w_v_t, seq=seq, name="projt_v")
    r3 = lambda a: a.reshape(batch, seq, -1)
    out_c = _mla(qn_t, qr_t, kn, k_rope, v_t, batch=batch, seq=seq)
    out_d = _rglru(r3(po), conv_w, conv_b, w_a, b_a, w_x, b_x, lam, batch=batch, seq=seq).reshape(batch * seq, -1)
    mix = jnp.concatenate([out_c, out_d], axis=-1)
    x = _proj(mix, w_out, seq=seq, res=x, name="proj_out")
    return _moe(x, norm_ffn, w_router, w_gate, w_up, w_down, layer)


def kernel(x, ev_norm_mix, ev_w_in, ev_w_out, ev_norm_ffn, ffn_w_gate, ffn_w_up, ffn_w_down, od_norm_mix, od_w_in, mla_q_norm, mla_w_uq, mla_kv_norm, mla_w_ukv, rg_conv_w, rg_conv_b, rg_w_a, rg_b_a, rg_w_x, rg_b_x, rg_lambda, od_w_out, od_norm_ffn, moe_router, moe_w_gate, moe_w_up, moe_w_down, final_norm):
    batch, seq, d = x.shape
    depth = ev_w_in.shape[0] + od_w_in.shape[0]
    assert d == D_MODEL and seq % KEY_CHUNK == 0 and (batch * seq) % FFN_ROW_TILE == 0
    tabs64 = _rope_tables(seq, HEAD_DIM, ROT_DIM)
    tabs32 = _rope_tables(seq, IDX_DIM, IDX_ROT)
    tabs_mla = _rope_tables(seq, C_ROPE, C_ROPE)
    h = x.reshape(batch * seq, d)
    for layer in range(depth):
        i = layer // 2
        if layer % 2 == 0:
            h = _even_layer(h, batch, seq, i, ev_norm_mix[i], ev_w_in[i], ev_w_out[i], ev_norm_ffn[i],
                            ffn_w_gate, ffn_w_up, ffn_w_down, tabs64, tabs32)
        else:
            h = _odd_layer(h, batch, seq, i, od_norm_mix[i], od_w_in[i], mla_q_norm[i], mla_w_uq[i], mla_kv_norm[i],
                           mla_w_ukv[i], rg_conv_w[i], rg_conv_b[i], rg_w_a[i], rg_b_a[i], rg_w_x[i], rg_b_x[i],
                           rg_lambda[i], od_w_out[i], od_norm_ffn[i], moe_router[i], moe_w_gate, moe_w_up,
                           moe_w_down, tabs_mla)
    out = _rmsnorm(h, final_norm, out_dtype=jnp.float32, name="rmsnorm_final")
    return out.reshape(batch, seq, d)
```

```python
import functools

import jax
import jax.numpy as jnp
import numpy as np
from jax import lax
from jax.experimental import pallas as pl
from jax.experimental.pallas import tpu as pltpu

D_MODEL = 1024
HEAD_DIM = 64
ROT_DIM = HEAD_DIM // 4
ROPE_THETA = 500000.0
NORM_EPS = 1e-6

A_HEADS = 8
A_KV_HEADS = 2
IDX_HEADS = 8
IDX_DIM = 32
IDX_ROT = IDX_DIM // 4
TOPK_MAX = 256

B_HEADS = 8
DILATED_PATTERNS = ((128, 1), (512, 4), (2048, 16))

C_HEADS = 8
C_NOPE = 64
C_ROPE = 32
C_V = 64
Q_LORA = 256
KV_LORA = 128

D_RNN = 512
RG_BLOCKS = 8
RG_BW = D_RNN // RG_BLOCKS
CONV_W = 4
RG_C = 8.0

D_FF = 3584
N_EXPERTS = 8
TOP_K = 2

LANES = 128
VMEM_LIMIT_BYTES = 56 * 1024 * 1024
BLOCK_Q = 128
KEY_CHUNK = 512
MLA_BLOCK_Q = 256
DSA_BLOCK_Q = 256
SUBLANES = 8
LOG2_E = 1.4426950408889634
ROW_TILE = 512
FFN_ROW_TILE = 1024
FFN_COL_TILE = 512
MOE_ROW_TILE = 1024
GATHER_ROWS = 256
RG_CHUNK = 256
NEG_BIG = -1e30
INT_MIN = -2147483648

_NT = (((1,), (1,)), ((), ()))


def _params(semantics, vmem=VMEM_LIMIT_BYTES):
    return pltpu.CompilerParams(dimension_semantics=semantics, vmem_limit_bytes=vmem)


def _bf16(a):
    return a if a.dtype == jnp.bfloat16 else a.astype(jnp.bfloat16)


def _rmsnorm_kernel(x_ref, g_ref, o_ref):
    x = x_ref[...].astype(jnp.float32)
    ms = jnp.mean(x * x, axis=-1, keepdims=True)
    o_ref[...] = (x * lax.rsqrt(ms + NORM_EPS) * g_ref[...]).astype(o_ref.dtype)


def _rmsnorm(x, g, *, width=None, col_block=0, out_dtype=jnp.bfloat16, name="rmsnorm"):
    n = x.shape[0]
    width = x.shape[1] if width is None else width
    return pl.pallas_call(
        _rmsnorm_kernel,
        grid=(n // ROW_TILE,),
        in_specs=[pl.BlockSpec((ROW_TILE, width), lambda i: (i, col_block)),
                  pl.BlockSpec((1, width), lambda i: (0, 0))],
        out_specs=pl.BlockSpec((ROW_TILE, width), lambda i: (i, 0)),
        out_shape=jax.ShapeDtypeStruct((n, width), out_dtype),
        compiler_params=_params(("parallel",)),
        name=name,
    )(x, g.reshape(1, width).astype(jnp.float32))


def _rope_tables(seq, period, rot_dim):
    half = rot_dim // 2
    inv_freq = 1.0 / (ROPE_THETA ** (jnp.arange(half, dtype=jnp.float32) * (2.0 / rot_dim)))
    ang = jnp.arange(seq, dtype=jnp.float32)[:, None] * inv_freq[None, :]
    cos, sin = jnp.cos(ang), jnp.sin(ang)
    rest = period - rot_dim
    ones = jnp.ones((seq, rest), jnp.float32)
    zeros = jnp.zeros((seq, rest), jnp.float32)
    zh = jnp.zeros((seq, half), jnp.float32)
    c = jnp.concatenate([cos, cos, ones], axis=1)
    sa = jnp.concatenate([-sin, zh, zeros], axis=1)
    sb = jnp.concatenate([zh, sin, zeros], axis=1)
    reps = LANES // period
    return tuple(jnp.tile(t, (1, reps)) for t in (c, sa, sb))


def _proj_kernel(*refs, rope_half, has_res):
    a_ref, w_ref = refs[0], refs[1]
    o_ref = refs[-1]
    y = jnp.dot(_bf16(a_ref[...]), _bf16(w_ref[...]), preferred_element_type=jnp.float32)
    pos = 2
    if rope_half:
        c_ref, sa_ref, sb_ref = refs[2:5]
        pos = 5
        tn = y.shape[1]
        reps = tn // LANES
        c = jnp.tile(c_ref[...], (1, reps))
        sa = jnp.tile(sa_ref[...], (1, reps))
        sb = jnp.tile(sb_ref[...], (1, reps))
        y = y * c + pltpu.roll(y, tn - rope_half, 1) * sa + pltpu.roll(y, rope_half, 1) * sb
    if has_res:
        y = y + refs[pos][...]
    o_ref[...] = y.astype(o_ref.dtype)


def _proj(a, w, *, seq, rope=None, rope_half=0, res=None, out_dtype=jnp.float32, name="proj"):
    n, k = a.shape
    np_ = w.shape[1]
    tm = ROW_TILE
    seq_blocks = seq // tm
    in_specs = [pl.BlockSpec((tm, k), lambda i: (i, 0)),
                pl.BlockSpec((k, np_), lambda i: (0, 0))]
    args = [a, w]
    if rope is not None:
        for t in rope:
            in_specs.append(pl.BlockSpec((tm, LANES), lambda i: (i % seq_blocks, 0)))
            args.append(t)
    if res is not None:
        in_specs.append(pl.BlockSpec((tm, np_), lambda i: (i, 0)))
        args.append(res)
    return pl.pallas_call(
        functools.partial(_proj_kernel, rope_half=rope_half if rope is not None else 0, has_res=res is not None),
        grid=(n // tm,),
        in_specs=in_specs,
        out_specs=pl.BlockSpec((tm, np_), lambda i: (i, 0)),
        out_shape=jax.ShapeDtypeStruct((n, np_), out_dtype),
        compiler_params=_params(("parallel",)),
        name=name,
    )(*args)


def _projt_kernel(*refs, rope_half, scale):
    wt_ref, a_ref = refs[0], refs[1]
    o_ref = refs[-1]
    y = lax.dot_general(_bf16(wt_ref[...]), _bf16(a_ref[...]), _NT, preferred_element_type=jnp.float32)
    if rope_half:
        c_ref, sa_ref, sb_ref = refs[2:5]
        rows = y.shape[0]
        reps = rows // LANES
        c = jnp.tile(c_ref[...], (reps, 1))
        sa = jnp.tile(sa_ref[...], (reps, 1))
        sb = jnp.tile(sb_ref[...], (reps, 1))
        y = y * c + pltpu.roll(y, rows - rope_half, 0) * sa + pltpu.roll(y, rope_half, 0) * sb
    if scale != 1.0:
        y = y * scale
    o_ref[0] = y.astype(o_ref.dtype)


def _projt(a, wt, *, seq, rope=None, rope_half=0, scale=1.0, out_dtype=jnp.bfloat16, name="projt"):
    n, k = a.shape
    np_ = wt.shape[0]
    tm = ROW_TILE
    seq_blocks = seq // tm
    in_specs = [pl.BlockSpec((np_, k), lambda i: (0, 0)),
                pl.BlockSpec((tm, k), lambda i: (i, 0))]
    args = [wt, a]
    if rope is not None:
        for t in rope:
            in_specs.append(pl.BlockSpec((LANES, tm), lambda i: (0, i % seq_blocks)))
            args.append(t)
    return pl.pallas_call(
        functools.partial(_projt_kernel, rope_half=rope_half if rope is not None else 0, scale=scale),
        grid=(n // tm,),
        in_specs=in_specs,
        out_specs=pl.BlockSpec((1, np_, tm), lambda i: (i, 0, 0)),
        out_shape=jax.ShapeDtypeStruct((n // tm, np_, tm), out_dtype),
        compiler_params=_params(("parallel",)),
        name=name,
    )(*args)


def _silu(g):
    return g * (1.0 / (1.0 + jnp.exp(-g)))


def _ffn_kernel(xn_ref, wg_ref, wu_ref, wd_ref, res_ref, o_ref, acc_ref):
    f = pl.program_id(1)

    @pl.when(f == 0)
    def _():
        acc_ref[...] = jnp.zeros_like(acc_ref)

    xn = xn_ref[...]
    g = jnp.dot(xn, _bf16(wg_ref[0]), preferred_element_type=jnp.float32)
    u = jnp.dot(xn, _bf16(wu_ref[0]), preferred_element_type=jnp.float32)
    h = (_silu(g) * u).astype(jnp.bfloat16)
    acc_ref[...] += jnp.dot(h, _bf16(wd_ref[0]), preferred_element_type=jnp.float32)

    @pl.when(f == pl.num_programs(1) - 1)
    def _():
        o_ref[...] = res_ref[...] + acc_ref[...]


def _ffn(xn, w_gate, w_up, w_down, res, layer):
    n, d = xn.shape
    ff = w_gate.shape[2]
    tm, tf = FFN_ROW_TILE, FFN_COL_TILE
    return pl.pallas_call(
        _ffn_kernel,
        grid=(n // tm, ff // tf),
        in_specs=[pl.BlockSpec((tm, d), lambda i, f: (i, 0)),
                  pl.BlockSpec((1, d, tf), lambda i, f: (layer, 0, f)),
                  pl.BlockSpec((1, d, tf), lambda i, f: (layer, 0, f)),
                  pl.BlockSpec((1, tf, d), lambda i, f: (layer, f, 0)),
                  pl.BlockSpec((tm, d), lambda i, f: (i, 0))],
        out_specs=pl.BlockSpec((tm, d), lambda i, f: (i, 0)),
        out_shape=jax.ShapeDtypeStruct((n, d), jnp.float32),
        scratch_shapes=[pltpu.VMEM((tm, d), jnp.float32)],
        compiler_params=_params(("parallel", "arbitrary")),
        name="ffn_swiglu",
    )(xn, w_gate, w_up, w_down, res)


def _dsa_kernel(qt_ref, k_ref, vt_ref, qit_ref, ki_ref, wt_ref, o_ref, key_scr, bias_scr, *scr, topk):
    acc_scr, m_scr, l_scr = scr[:A_HEADS], scr[A_HEADS:2 * A_HEADS], scr[2 * A_HEADS:]
    i = pl.program_id(1)
    tq = DSA_BLOCK_Q
    n_chunks = (i * tq + tq + KEY_CHUNK - 1) // KEY_CHUNK
    qpos = i * tq + lax.broadcasted_iota(jnp.int32, (1, tq), 1)
    kloc = lax.broadcasted_iota(jnp.int32, (KEY_CHUNK, 1), 0)

    w = wt_ref[0]

    def index_body(c, carry):
        kc = ki_ref[pl.ds(c * KEY_CHUNK, KEY_CHUNK), :][:, :IDX_DIM]
        score = jnp.zeros((KEY_CHUNK, tq), jnp.float32)
        for h in range(IDX_HEADS):
            lg = jnp.dot(kc, qit_ref[0, h * IDX_DIM:(h + 1) * IDX_DIM, :], preferred_element_type=jnp.float32)
            score = score + w[h:h + 1, :] * jnp.maximum(lg, 0.0)
        bits = pltpu.bitcast(score, jnp.int32)
        key = jnp.where(bits < 0, bits ^ jnp.int32(0x7FFFFFFF), bits)
        key_scr[c] = jnp.where(c * KEY_CHUNK + kloc <= qpos, key, jnp.int32(INT_MIN))
        return carry

    lax.fori_loop(0, n_chunks, index_body, 0)

    def count(pred):
        def body(c, acc):
            hit = pred(key_scr[c]).astype(jnp.int32)
            return acc + jnp.sum(hit.reshape(KEY_CHUNK // SUBLANES, SUBLANES, tq), axis=0)
        acc = lax.fori_loop(0, n_chunks, body, jnp.zeros((SUBLANES, tq), jnp.int32))
        return jnp.sum(acc, axis=0, keepdims=True)

    def bit_body(t, res):
        cand = res | jnp.left_shift(jnp.int32(1), 31 - t)
        cand_key = cand ^ jnp.int32(INT_MIN)
        return jnp.where(count(lambda key: key >= cand_key) >= topk, cand, res)

    thr = lax.fori_loop(0, 32, bit_body, jnp.zeros((1, tq), jnp.int32)) ^ jnp.int32(INT_MIN)
    need = (topk - count(lambda key: key > thr)).astype(jnp.float32)

    tri = (lax.broadcasted_iota(jnp.int32, (KEY_CHUNK, KEY_CHUNK), 1)
           <= lax.broadcasted_iota(jnp.int32, (KEY_CHUNK, KEY_CHUNK), 0)).astype(jnp.bfloat16)

    def mask_body(c, seen):
        key = key_scr[c]
        eq = key == thr
        running = jnp.dot(tri, eq.astype(jnp.bfloat16), preferred_element_type=jnp.float32) + seen
        sel = ((key > thr) | (eq & (running <= need))) & (c * KEY_CHUNK + kloc <= qpos)
        bias_scr[c] = jnp.where(sel, 0.0, NEG_BIG)
        return running[KEY_CHUNK - 1:KEY_CHUNK, :]

    lax.fori_loop(0, n_chunks, mask_body, jnp.zeros((1, tq), jnp.float32))

    for h in range(A_HEADS):
        acc_scr[h][...] = jnp.zeros_like(acc_scr[h])
        l_scr[h][...] = jnp.zeros_like(l_scr[h])
        m_scr[h][...] = jnp.full_like(m_scr[h], NEG_BIG)
    group = A_HEADS // A_KV_HEADS

    def att_body(c, carry):
        rows = pl.ds(c * KEY_CHUNK, KEY_CHUNK)
        bias = bias_scr[c]
        scores = []
        for h in range(A_HEADS):
            n = h // group
            scores.append(jnp.dot(k_ref[rows, n * HEAD_DIM:(n + 1) * HEAD_DIM],
                                  qt_ref[0, h * HEAD_DIM:(h + 1) * HEAD_DIM, :],
                                  preferred_element_type=jnp.float32))
        probs, alphas = [], []
        for h in range(A_HEADS):
            s = scores[h] + bias
            m_old = m_scr[h][...]
            m_new = jnp.maximum(m_old, jnp.max(s, axis=0, keepdims=True))
            alpha = jnp.exp2(m_old - m_new)
            p = jnp.exp2(s - m_new)
            l_scr[h][...] = l_scr[h][...] * alpha + jnp.sum(p, axis=0, keepdims=True)
            m_scr[h][...] = m_new
            probs.append(p.astype(jnp.bfloat16))
            alphas.append(alpha)
        for h in range(A_HEADS):
            n = h // group
            acc_scr[h][...] = acc_scr[h][...] * alphas[h] + jnp.dot(
                vt_ref[c, n * HEAD_DIM:(n + 1) * HEAD_DIM, :], probs[h], preferred_element_type=jnp.float32)
        return carry

    lax.fori_loop(0, n_chunks, att_body, 0)
    out_t = jnp.concatenate([acc_scr[h][...] * (1.0 / l_scr[h][...]) for h in range(A_HEADS)], axis=0)
    o_ref[...] = out_t.T


def _dsa(qa_t, ka, va_t, qi_t, ki, w_t, *, batch, seq):
    topk = min(TOPK_MAX, seq // 4)
    tq = DSA_BLOCK_Q
    tiles = seq // ROW_TILE
    per_tile = ROW_TILE // tq
    n_chunks = seq // KEY_CHUNK
    qw = A_HEADS * HEAD_DIM
    kvw = A_KV_HEADS * HEAD_DIM
    q_tile = lambda b, i: (b * tiles + i // per_tile, 0, i % per_tile)
    return pl.pallas_call(
        functools.partial(_dsa_kernel, topk=topk),
        grid=(batch, seq // tq),
        in_specs=[pl.BlockSpec((1, qw, tq), q_tile),
                  pl.BlockSpec((seq, kvw), lambda b, i: (b, 0)),
                  pl.BlockSpec((tiles, kvw, ROW_TILE), lambda b, i: (b, 0, 0)),
                  pl.BlockSpec((1, IDX_HEADS * IDX_DIM, tq), q_tile),
                  pl.BlockSpec((seq, LANES), lambda b, i: (b, 0)),
                  pl.BlockSpec((1, IDX_HEADS, tq), q_tile)],
        out_specs=pl.BlockSpec((tq, qw), lambda b, i: (b * (seq // tq) + i, 0)),
        out_shape=jax.ShapeDtypeStruct((batch * seq, qw), jnp.float32),
        scratch_shapes=([pltpu.VMEM((n_chunks, KEY_CHUNK, tq), jnp.int32),
                         pltpu.VMEM((n_chunks, KEY_CHUNK, tq), jnp.float32)]
                        + [pltpu.VMEM((HEAD_DIM, tq), jnp.float32)] * A_HEADS
                        + [pltpu.VMEM((1, tq), jnp.float32)] * (2 * A_HEADS)),
        compiler_params=_params(("parallel", "arbitrary")),
        name="dsa_attention",
    )(qa_t, ka, va_t, qi_t, ki, w_t)


def _dilated_kernel(*refs, seq):
    pairs = B_HEADS // 2
    q_refs, k_refs, v_refs = refs[:pairs], refs[pairs:2 * pairs], refs[2 * pairs:3 * pairs]
    o_ref = refs[3 * pairs]
    scr = refs[3 * pairs + 1:]
    acc_scr, m_scr, l_scr = scr[:pairs], scr[pairs:2 * pairs], scr[2 * pairs:]
    scale = HEAD_DIM ** -0.5
    row = lax.broadcasted_iota(jnp.int32, (BLOCK_Q, BLOCK_Q), 0)
    col = lax.broadcasted_iota(jnp.int32, (BLOCK_Q, BLOCK_Q), 1)
    mask_cur = col <= row
    mask_prev = col >= row

    for p in range(pairs):
        acc_scr[p][...] = jnp.zeros_like(acc_scr[p])
        l_scr[p][...] = jnp.zeros_like(l_scr[p])
        m_scr[p][...] = jnp.full_like(m_scr[p], NEG_BIG)

    def gather(ref_list, rows):
        return jnp.concatenate([_bf16(r[0, rows, :]) for r in ref_list], axis=1)

    def spread(cols):
        return jnp.concatenate([jnp.broadcast_to(c, (BLOCK_Q, HEAD_DIM)) for c in cols], axis=1)

    for (window, dil) in DILATED_PATTERNS:
        assert window // dil == BLOCK_Q and seq % (dil * BLOCK_Q) == 0
        n_blocks = seq // (dil * BLOCK_Q)

        def block_body(idx, carry, dil=dil, n_blocks=n_blocks):
            phase = idx // n_blocks
            blk = idx % n_blocks
            def view_rows(block):
                if dil == 1:
                    return pl.ds(pl.multiple_of(BLOCK_Q * block, BLOCK_Q), BLOCK_Q)
                return pl.ds(phase + dil * BLOCK_Q * block, BLOCK_Q, stride=dil)

            rows = view_rows(blk)
            q2 = gather(q_refs, rows)
            k_cur = gather(k_refs, rows)
            v_cur = gather(v_refs, rows)
            if n_blocks > 1:
                rows_p = view_rows(jnp.maximum(blk - 1, 0))
                k_prev = gather(k_refs, rows_p)
                v_prev = gather(v_refs, rows_p)
                mask_p = mask_prev & (blk > 0)
            m_old = jnp.concatenate([m[rows, :] for m in m_scr], axis=1)
            heads = q2.shape[1] // HEAD_DIM
            lanes = [slice(h * HEAD_DIM, (h + 1) * HEAD_DIM) for h in range(heads)]
            s_cur = [lax.dot_general(q2[:, sl], k_cur[:, sl], _NT, preferred_element_type=jnp.float32) for sl in lanes]
            if n_blocks > 1:
                s_prev = [lax.dot_general(q2[:, sl], k_prev[:, sl], _NT, preferred_element_type=jnp.float32)
                          for sl in lanes]
            new_m, new_sum, p_cur, p_prev = [], [], [], []
            for h in range(heads):
                s_c = jnp.where(mask_cur, s_cur[h] * scale, NEG_BIG)
                mx = jnp.max(s_c, axis=1, keepdims=True)
                if n_blocks > 1:
                    s_p = jnp.where(mask_p, s_prev[h] * scale, NEG_BIG)
                    mx = jnp.maximum(mx, jnp.max(s_p, axis=1, keepdims=True))
                m_h = jnp.maximum(m_old[:, h * HEAD_DIM:h * HEAD_DIM + 1], mx)
                p_c = jnp.exp(s_c - m_h)
                psum = jnp.sum(p_c, axis=1, keepdims=True)
                p_cur.append(p_c.astype(jnp.bfloat16))
                if n_blocks > 1:
                    p_p = jnp.exp(s_p - m_h)
                    psum = psum + jnp.sum(p_p, axis=1, keepdims=True)
                    p_prev.append(p_p.astype(jnp.bfloat16))
                new_m.append(m_h)
                new_sum.append(psum)
            new_pv = []
            for h in range(heads):
                pv = jnp.dot(p_cur[h], v_cur[:, lanes[h]], preferred_element_type=jnp.float32)
                if n_blocks > 1:
                    pv = pv + jnp.dot(p_prev[h], v_prev[:, lanes[h]], preferred_element_type=jnp.float32)
                new_pv.append(pv)
            m_new = spread(new_m)
            alpha = jnp.exp(m_old - m_new)
            sums = spread(new_sum)
            pvs = jnp.concatenate(new_pv, axis=1)
            for p in range(pairs):
                cols = slice(p * LANES, (p + 1) * LANES)
                m_scr[p][rows, :] = m_new[:, cols]
                l_scr[p][rows, :] = l_scr[p][rows, :] * alpha[:, cols] + sums[:, cols]
                acc_scr[p][rows, :] = acc_scr[p][rows, :] * alpha[:, cols] + pvs[:, cols]
            return carry

        lax.fori_loop(0, dil * n_blocks, block_body, 0)

    for p in range(pairs):
        o_ref[0, :, p * LANES:(p + 1) * LANES] = acc_scr[p][...] * (1.0 / l_scr[p][...])


def _dilated(qk, v, *, batch, seq):
    width = B_HEADS * HEAD_DIM
    pairs = B_HEADS // 2

    def col_block(j):
        return pl.BlockSpec((1, seq, LANES), lambda b: (b, 0, j))

    return pl.pallas_call(
        functools.partial(_dilated_kernel, seq=seq),
        grid=(batch,),
        in_specs=([col_block(p) for p in range(pairs)] + [col_block(pairs + p) for p in range(pairs)]
                  + [col_block(p) for p in range(pairs)]),
        out_specs=pl.BlockSpec((1, seq, width), lambda b: (b, 0, 0)),
        out_shape=jax.ShapeDtypeStruct((batch, seq, width), jnp.float32),
        scratch_shapes=[pltpu.VMEM((seq, LANES), jnp.float32)] * (3 * pairs),
        compiler_params=_params(("parallel",)),
        name="dilated_attention",
    )(*([qk] * (2 * pairs) + [v] * pairs))


def _mla_kernel(qn_ref, qr_ref, kn_ref, kr_ref, vt_ref, o_ref, *scr):
    acc_scr, m_scr, l_scr = scr[:C_HEADS], scr[C_HEADS:2 * C_HEADS], scr[2 * C_HEADS:]
    i = pl.program_id(1)
    tq = MLA_BLOCK_Q
    n_full = (i * tq + 1) // KEY_CHUNK
    n_all = (i * tq + tq + KEY_CHUNK - 1) // KEY_CHUNK
    qpos = i * tq + lax.broadcasted_iota(jnp.int32, (1, tq), 1)
    kloc = lax.broadcasted_iota(jnp.int32, (KEY_CHUNK, 1), 0)

    for h in range(C_HEADS):
        acc_scr[h][...] = jnp.zeros_like(acc_scr[h])
        l_scr[h][...] = jnp.zeros_like(l_scr[h])
        m_scr[h][...] = jnp.full_like(m_scr[h], NEG_BIG)

    def chunk(c, carry, masked):
        rows = pl.ds(c * KEY_CHUNK, KEY_CHUNK)
        kr = kr_ref[rows, :][:, :C_ROPE]
        visible = (c * KEY_CHUNK + kloc) <= qpos
        scores = []
        for h in range(C_HEADS):
            hn = slice(h * C_NOPE, (h + 1) * C_NOPE)
            scores.append(jnp.dot(kn_ref[rows, hn], qn_ref[0, hn, :], preferred_element_type=jnp.float32)
                          + jnp.dot(kr, qr_ref[0, h * C_ROPE:(h + 1) * C_ROPE, :],
                                    preferred_element_type=jnp.float32))
        probs, alphas = [], []
        for h in range(C_HEADS):
            s = jnp.where(visible, scores[h], NEG_BIG) if masked else scores[h]
            m_old = m_scr[h][...]
            m_new = jnp.maximum(m_old, jnp.max(s, axis=0, keepdims=True))
            alpha = jnp.exp2(m_old - m_new)
            p = jnp.exp2(s - m_new)
            l_scr[h][...] = l_scr[h][...] * alpha + jnp.sum(p, axis=0, keepdims=True)
            m_scr[h][...] = m_new
            probs.append(p.astype(jnp.bfloat16))
            alphas.append(alpha)
        for h in range(C_HEADS):
            hv = slice(h * C_V, (h + 1) * C_V)
            acc_scr[h][...] = acc_scr[h][...] * alphas[h] + jnp.dot(vt_ref[c, hv, :], probs[h],
                                                                     preferred_element_type=jnp.float32)
        return carry

    lax.fori_loop(0, n_full, functools.partial(chunk, masked=False), 0)
    lax.fori_loop(n_full, n_all, functools.partial(chunk, masked=True), 0)

    out_t = jnp.concatenate([acc_scr[h][...] * (1.0 / l_scr[h][...]) for h in range(C_HEADS)], axis=0)
    o_ref[...] = out_t.T


def _mla(qn_t, qr_t, kn, kr, v_t, *, batch, seq):
    tq = MLA_BLOCK_Q
    tiles = seq // ROW_TILE
    per_tile = ROW_TILE // tq
    kw = C_HEADS * C_NOPE
    vw = C_HEADS * C_V
    n = batch * seq
    return pl.pallas_call(
        _mla_kernel,
        grid=(batch, seq // tq),
        in_specs=[pl.BlockSpec((1, kw, tq), lambda b, i: (b * tiles + i // per_tile, 0, i % per_tile)),
                  pl.BlockSpec((1, C_HEADS * C_ROPE, tq), lambda b, i: (b * tiles + i // per_tile, 0, i % per_tile)),
                  pl.BlockSpec((seq, kw), lambda b, i: (b, 0)),
                  pl.BlockSpec((seq, LANES), lambda b, i: (b, 0)),
                  pl.BlockSpec((tiles, vw, ROW_TILE), lambda b, i: (b, 0, 0))],
        out_specs=pl.BlockSpec((tq, vw), lambda b, i: (b * (seq // tq) + i, 0)),
        out_shape=jax.ShapeDtypeStruct((n, vw), jnp.float32),
        scratch_shapes=([pltpu.VMEM((C_V, tq), jnp.float32)] * C_HEADS
                        + [pltpu.VMEM((1, tq), jnp.float32)] * (2 * C_HEADS)),
        compiler_params=_params(("parallel", "arbitrary")),
        name="mla_attention",
    )(qn_t, qr_t, kn, kr, v_t)


def _expm1(y):
    u = jnp.exp(y)
    safe = jnp.where(u == 1.0, 2.0, u)
    return jnp.where(u == 1.0, y, (u - 1.0) * y / jnp.log(safe))


def _gelu_tanh(x):
    return 0.5 * x * (1.0 + jnp.tanh(np.sqrt(2.0 / np.pi).astype(np.float32) * (x + 0.044715 * (x * x * x))))


def _rglru_kernel(x_ref, g_ref, cw_ref, cb_ref, wa_ref, ba_ref, wx_ref, bx_ref, lam_ref, o_ref, a_scr, u_scr, *, seq):
    cw = cw_ref[...]
    neg_c_softplus = -RG_C * jnp.log1p(jnp.exp(-jnp.abs(-lam_ref[...]))) - RG_C * jnp.maximum(-lam_ref[...], 0.0)
    hi = lax.Precision.HIGHEST
    sub = 8

    def chunk_body(ci, h):
        r0 = pl.multiple_of(ci * RG_CHUNK, RG_CHUNK)
        xa = x_ref[0, pl.ds(r0, RG_CHUNK), :]
        prev = x_ref[0, pl.ds(pl.multiple_of(jnp.maximum(r0 - sub, 0), sub), sub), :]
        prev = jnp.where(ci > 0, prev, 0.0)
        xcat = jnp.concatenate([prev, xa], axis=0)
        xc = cw[CONV_W - 1:CONV_W, :] * xa + cb_ref[...]
        for j in range(1, CONV_W):
            xc = xc + cw[CONV_W - 1 - j:CONV_W - j, :] * pltpu.roll(xcat, j, 0)[sub:, :]
        r = 1.0 / (1.0 + jnp.exp(-(jnp.dot(xc, wa_ref[...], precision=hi, preferred_element_type=jnp.float32)
                                   + ba_ref[...])))
        ig = 1.0 / (1.0 + jnp.exp(-(jnp.dot(xc, wx_ref[...], precision=hi, preferred_element_type=jnp.float32)
                                    + bx_ref[...])))
        log_a = neg_c_softplus * r
        a_scr[...] = jnp.exp(log_a)
        u_scr[...] = jnp.sqrt(-_expm1(2.0 * log_a)) * (ig * xc)
        gate = _gelu_tanh(g_ref[0, pl.ds(r0, RG_CHUNK), :])

        def group_body(gi, h):
            g0 = pl.multiple_of(gi * sub, sub)
            a8 = a_scr[pl.ds(g0, sub), :]
            u8 = u_scr[pl.ds(g0, sub), :]
            rows = []
            for t in range(sub):
                h = a8[t:t + 1, :] * h + u8[t:t + 1, :]
                rows.append(h)
            u_scr[pl.ds(g0, sub), :] = jnp.concatenate(rows, axis=0)
            return h

        h = lax.fori_loop(0, RG_CHUNK // sub, group_body, h)
        o_ref[0, pl.ds(r0, RG_CHUNK), :] = u_scr[...] * gate
        return h

    lax.fori_loop(0, seq // RG_CHUNK, chunk_body, jnp.zeros((1, D_RNN), jnp.float32))


def _block_diag(w):
    nb, bw, _ = w.shape
    eye = jnp.eye(nb, dtype=w.dtype)
    return (eye[:, None, :, None] * w[:, :, None, :]).reshape(nb * bw, nb * bw)


def _rglru(po, conv_w, conv_b, w_a, b_a, w_x, b_x, lam, *, batch, seq):
    row = lambda a: a.reshape(1, D_RNN).astype(jnp.float32)
    vec = pl.BlockSpec((1, D_RNN), lambda b: (0, 0))
    mat = pl.BlockSpec((D_RNN, D_RNN), lambda b: (0, 0))
    return pl.pallas_call(
        functools.partial(_rglru_kernel, seq=seq),
        grid=(batch,),
        in_specs=[pl.BlockSpec((1, seq, D_RNN), lambda b: (b, 0, 0)),
                  pl.BlockSpec((1, seq, D_RNN), lambda b: (b, 0, 1)),
                  pl.BlockSpec((CONV_W, D_RNN), lambda b: (0, 0)),
                  vec, mat, vec, mat, vec, vec],
        out_specs=pl.BlockSpec((1, seq, D_RNN), lambda b: (b, 0, 0)),
        out_shape=jax.ShapeDtypeStruct((batch, seq, D_RNN), jnp.float32),
        scratch_shapes=[pltpu.VMEM((RG_CHUNK, D_RNN), jnp.float32)] * 2,
        compiler_params=_params(("parallel",)),
        name="rglru",
    )(po, po, conv_w.astype(jnp.float32), row(conv_b), _block_diag(w_a), row(b_a), _block_diag(w_x), row(b_x), row(lam))


def _router_kernel(x_ref, g_ref, wr_ref, xn_ref, idx_ref, gate_ref):
    x = x_ref[...]
    ms = jnp.mean(x * x, axis=-1, keepdims=True)
    xn = x * lax.rsqrt(ms + NORM_EPS) * g_ref[...]
    xn_ref[...] = xn
    logits = jnp.dot(xn, wr_ref[...], precision=lax.Precision.HIGHEST, preferred_element_type=jnp.float32)
    lane = lax.broadcasted_iota(jnp.int32, logits.shape, 1)
    logits = jnp.where(lane < N_EXPERTS, logits, -jnp.inf)
    m1 = jnp.max(logits, axis=1, keepdims=True)
    i1 = jnp.min(jnp.where(logits == m1, lane, LANES), axis=1, keepdims=True)
    rest = jnp.where(lane == i1, -jnp.inf, logits)
    m2 = jnp.max(rest, axis=1, keepdims=True)
    i2 = jnp.min(jnp.where(rest == m2, lane, LANES), axis=1, keepdims=True)
    e = jnp.exp(m2 - m1)
    g1 = 1.0 / (1.0 + e)
    g2 = e / (1.0 + e)
    idx_ref[...] = jnp.where(lane == 0, i1, jnp.where(lane == 1, i2, 0))
    gate_ref[...] = jnp.where(lane == 0, g1, jnp.where(lane == 1, g2, 0.0))


def _router(x, g, w_router):
    n, d = x.shape
    wr = jnp.pad(w_router.astype(jnp.float32), ((0, 0), (0, LANES - N_EXPERTS)))
    tm = ROW_TILE
    return pl.pallas_call(
        _router_kernel,
        grid=(n // tm,),
        in_specs=[pl.BlockSpec((tm, d), lambda i: (i, 0)),
                  pl.BlockSpec((1, d), lambda i: (0, 0)),
                  pl.BlockSpec((d, LANES), lambda i: (0, 0))],
        out_specs=[pl.BlockSpec((tm, d), lambda i: (i, 0)),
                   pl.BlockSpec((tm, LANES), lambda i: (i, 0)),
                   pl.BlockSpec((tm, LANES), lambda i: (i, 0))],
        out_shape=[jax.ShapeDtypeStruct((n, d), jnp.float32),
                   jax.ShapeDtypeStruct((n, LANES), jnp.int32),
                   jax.ShapeDtypeStruct((n, LANES), jnp.float32)],
        compiler_params=_params(("parallel",)),
        name="moe_router",
    )(x, g.reshape(1, d).astype(jnp.float32), wr)


def _row_copy(src_hbm, dst, src_row, dst_row, sem):
    return pltpu.make_async_copy(src_hbm.at[pl.ds(src_row, 1)], dst.at[pl.ds(dst_row, 1)], sem)


def _gather_kernel(tok_ref, x_hbm, o_ref, sem):
    base = pl.program_id(0) * GATHER_ROWS

    def issue(r, carry):
        _row_copy(x_hbm, o_ref, tok_ref[base + r], r, sem).start()
        return carry

    lax.fori_loop(0, GATHER_ROWS, issue, 0)
    pltpu.make_async_copy(x_hbm.at[pl.ds(0, GATHER_ROWS)], o_ref, sem).wait()


def _gather_rows(row_tok, x):
    n_rows = row_tok.shape[0]
    d = x.shape[1]
    return pl.pallas_call(
        _gather_kernel,
        grid_spec=pltpu.PrefetchScalarGridSpec(
            num_scalar_prefetch=1,
            grid=(n_rows // GATHER_ROWS,),
            in_specs=[pl.BlockSpec(memory_space=pl.ANY)],
            out_specs=pl.BlockSpec((GATHER_ROWS, d), lambda i, tok: (i, 0)),
            scratch_shapes=[pltpu.SemaphoreType.DMA(())]),
        out_shape=jax.ShapeDtypeStruct((n_rows, d), x.dtype),
        compiler_params=_params(("arbitrary",)),
        name="moe_gather",
    )(row_tok, x)


def _experts_kernel(be_ref, nv_ref, x_ref, wg_ref, wu_ref, wd_ref, o_ref, acc_ref):
    i = pl.program_id(0)
    f = pl.program_id(1)
    last = pl.num_programs(1) - 1

    @pl.when(i < nv_ref[0])
    def _():
        @pl.when(f == 0)
        def _():
            acc_ref[...] = jnp.zeros_like(acc_ref)

        xb = _bf16(x_ref[...])
        g = jnp.dot(xb, _bf16(wg_ref[0]), preferred_element_type=jnp.float32)
        u = jnp.dot(xb, _bf16(wu_ref[0]), preferred_element_type=jnp.float32)
        h = (_silu(g) * u).astype(jnp.bfloat16)
        acc_ref[...] += jnp.dot(h, _bf16(wd_ref[0]), preferred_element_type=jnp.float32)

        @pl.when(f == last)
        def _():
            o_ref[...] = acc_ref[...]

    @pl.when((i >= nv_ref[0]) & (f == last))
    def _():
        o_ref[...] = jnp.zeros_like(o_ref)


def _experts(blk_expert, n_valid, x_rows, w_gate, w_up, w_down):
    n_rows, d = x_rows.shape
    ff = w_gate.shape[2]
    tm, tf = MOE_ROW_TILE, FFN_COL_TILE
    n_f = ff // tf

    def fcol(i, f, nv):
        return jnp.where(i < nv[0], f, n_f - 1)

    return pl.pallas_call(
        _experts_kernel,
        grid_spec=pltpu.PrefetchScalarGridSpec(
            num_scalar_prefetch=2,
            grid=(n_rows // tm, n_f),
            in_specs=[pl.BlockSpec((tm, d), lambda i, f, be, nv: (i, 0)),
                      pl.BlockSpec((1, d, tf), lambda i, f, be, nv: (be[i], 0, fcol(i, f, nv))),
                      pl.BlockSpec((1, d, tf), lambda i, f, be, nv: (be[i], 0, fcol(i, f, nv))),
                      pl.BlockSpec((1, tf, d), lambda i, f, be, nv: (be[i], fcol(i, f, nv), 0))],
            out_specs=pl.BlockSpec((tm, d), lambda i, f, be, nv: (i, 0)),
            scratch_shapes=[pltpu.VMEM((tm, d), jnp.float32)]),
        out_shape=jax.ShapeDtypeStruct((n_rows, d), jnp.float32),
        compiler_params=_params(("arbitrary", "arbitrary")),
        name="moe_experts",
    )(blk_expert, n_valid, x_rows, w_gate, w_up, w_down)


def _combine_kernel(dest_ref, y_hbm, x_ref, gate_ref, o_ref, buf, sem):
    base = pl.program_id(0) * GATHER_ROWS * TOP_K

    def issue(r, carry):
        for k in range(TOP_K):
            _row_copy(y_hbm, buf.at[k], dest_ref[base + r * TOP_K + k], r, sem).start()
        return carry

    lax.fori_loop(0, GATHER_ROWS, issue, 0)
    for k in range(TOP_K):
        pltpu.make_async_copy(y_hbm.at[pl.ds(0, GATHER_ROWS)], buf.at[k], sem).wait()
    gate = gate_ref[...]
    o_ref[...] = x_ref[...] + gate[:, 0:1] * buf[0] + gate[:, 1:2] * buf[1]


def _combine(dest, y_rows, x, gates):
    n, d = x.shape
    return pl.pallas_call(
        _combine_kernel,
        grid_spec=pltpu.PrefetchScalarGridSpec(
            num_scalar_prefetch=1,
            grid=(n // GATHER_ROWS,),
            in_specs=[pl.BlockSpec(memory_space=pl.ANY),
                      pl.BlockSpec((GATHER_ROWS, d), lambda i, dest: (i, 0)),
                      pl.BlockSpec((GATHER_ROWS, LANES), lambda i, dest: (i, 0))],
            out_specs=pl.BlockSpec((GATHER_ROWS, d), lambda i, dest: (i, 0)),
            scratch_shapes=[pltpu.VMEM((TOP_K, GATHER_ROWS, d), jnp.float32),
                            pltpu.SemaphoreType.DMA(())]),
        out_shape=jax.ShapeDtypeStruct((n, d), jnp.float32),
        compiler_params=_params(("arbitrary",)),
        name="moe_combine",
    )(dest, y_rows, x, gates)


def _moe(x, g, w_router, w_gate, w_up, w_down, layer):
    n = x.shape[0]
    w_gate, w_up, w_down = (w.reshape((-1,) + w.shape[2:]) for w in (w_gate, w_up, w_down))
    xn, idx, gates = _router(x, g, w_router)
    e_flat = idx[:, :TOP_K].reshape(-1)
    onehot = (e_flat[:, None] == jnp.arange(N_EXPERTS, dtype=jnp.int32)[None, :]).astype(jnp.int32)
    counts = jnp.sum(onehot, axis=0)
    rank = jnp.sum((jnp.cumsum(onehot, axis=0) - onehot) * onehot, axis=1)
    padded = (counts + MOE_ROW_TILE - 1) // MOE_ROW_TILE * MOE_ROW_TILE
    pad_end = jnp.cumsum(padded)
    pad_start = pad_end - padded
    dest = (pad_start[e_flat] + rank).astype(jnp.int32)
    n_rows = n * TOP_K + N_EXPERTS * MOE_ROW_TILE
    tok_flat = jnp.repeat(jnp.arange(n, dtype=jnp.int32), TOP_K)
    row_tok = jnp.zeros((n_rows,), jnp.int32).at[dest].set(tok_flat)
    n_blk = n_rows // MOE_ROW_TILE
    blk_expert = jnp.clip(jnp.searchsorted(pad_end, jnp.arange(n_blk, dtype=jnp.int32) * MOE_ROW_TILE, side='right'),
                          0, N_EXPERTS - 1).astype(jnp.int32)
    n_valid = (pad_end[-1] // MOE_ROW_TILE).astype(jnp.int32).reshape(1)
    x_rows = _gather_rows(row_tok, xn)
    y_rows = _experts(blk_expert + layer * N_EXPERTS, n_valid, x_rows, w_gate, w_up, w_down)
    return _combine(dest, y_rows, x, gates)


def _pad_cols(w, width):
    return jnp.pad(w, ((0, 0), (0, width - w.shape[1])))


def _even_layer(x, batch, seq, layer, norm_mix, w_in, w_out, norm_ffn, w_gate, w_up, w_down, tabs64, tabs32):
    qa, ka, va, qi, ki, wi, qb, kb, vb = jnp.split(
        w_in, np.cumsum([512, 128, 128, 256, 32, 8, 512, 512])[:].tolist(), axis=1)
    xn = _rmsnorm(x, norm_mix, name="rmsnorm_mix")
    tabs64_t = tuple(t.T for t in tabs64)
    tabs32_t = tuple(t.T for t in tabs32)
    half64, half32 = ROT_DIM // 2, IDX_ROT // 2
    bf = jnp.bfloat16
    qk_b = _proj(xn, jnp.concatenate([qb, kb], axis=1), seq=seq, rope=tabs64, rope_half=half64, name="proj_qk_b")
    v_b = _proj(xn, vb, seq=seq, name="proj_v_b")
    k_a = _proj(xn, ka, seq=seq, rope=tabs64, rope_half=half64, out_dtype=bf, name="proj_k_a")
    k_i = _proj(xn, _pad_cols(ki, LANES), seq=seq, rope=tabs32, rope_half=half32, out_dtype=bf, name="proj_k_idx")
    qa_t = _projt(xn, qa.T, seq=seq, rope=tabs64_t, rope_half=half64, scale=HEAD_DIM ** -0.5 * LOG2_E, name="projt_q_a")
    va_t = _projt(xn, va.T, seq=seq, name="projt_v_a")
    qi_t = _projt(xn, qi.T, seq=seq, rope=tabs32_t, rope_half=half32, name="projt_q_idx")
    w_t = _projt(xn, wi.T, seq=seq, scale=IDX_DIM ** -0.5 * IDX_HEADS ** -0.5, out_dtype=jnp.float32, name="projt_w_idx")
    out_a = _dsa(qa_t, k_a, va_t, qi_t, k_i, w_t, batch=batch, seq=seq)
    out_b = _dilated(qk_b.reshape(batch, seq, -1), v_b.reshape(batch, seq, -1), batch=batch, seq=seq)
    mix = jnp.concatenate([out_a, out_b.reshape(batch * seq, -1)], axis=-1)
    x = _proj(mix, w_out, seq=seq, res=x, name="proj_out")
    xn = _rmsnorm(x, norm_ffn, name="rmsnorm_ffn")
    return _ffn(xn, w_gate, w_up, w_down, x, layer)


def _odd_layer(x, batch, seq, layer, norm_mix, w_in, q_norm, w_uq, kv_norm, w_ukv, conv_w, conv_b, w_a, b_a, w_x, b_x, lam,
               w_out, norm_ffn, w_router, w_gate, w_up, w_down, tabs_mla):
    cq, ckv, kr, xr, gr = jnp.split(w_in, np.cumsum([Q_LORA, KV_LORA, C_ROPE, D_RNN]).tolist(), axis=1)
    w_plain = jnp.concatenate([xr, gr, cq, ckv], axis=1)
    xn = _rmsnorm(x, norm_mix, name="rmsnorm_mix")
    po = _proj(xn, w_plain, seq=seq, name="proj_plain")
    k_rope = _proj(xn, _pad_cols(kr, LANES), seq=seq, rope=tabs_mla, rope_half=C_ROPE // 2,
                   out_dtype=jnp.bfloat16, name="proj_krope")
    cqn = _rmsnorm(po, q_norm, width=Q_LORA, col_block=2 * D_RNN // Q_LORA, name="rmsnorm_cq")
    ckvn = _rmsnorm(po, kv_norm, width=KV_LORA, col_block=(2 * D_RNN + Q_LORA) // KV_LORA, name="rmsnorm_ckv")
    w_uq = w_uq.reshape(Q_LORA, C_HEADS, C_NOPE + C_ROPE)
    w_qn_t = w_uq[:, :, :C_NOPE].reshape(Q_LORA, C_HEADS * C_NOPE).T
    w_qr_t = w_uq[:, :, C_NOPE:].reshape(Q_LORA, C_HEADS * C_ROPE).T
    w_ukv = w_ukv.reshape(KV_LORA, C_HEADS, C_NOPE + C_V)
    w_kn = w_ukv[:, :, :C_NOPE].reshape(KV_LORA, C_HEADS * C_NOPE)
    w_v_t = w_ukv[:, :, C_NOPE:].reshape(KV_LORA, C_HEADS * C_V).T
    q_scale = (C_NOPE + C_ROPE) ** -0.5 * LOG2_E
    tabs_mla_t = tuple(t.T for t in tabs_mla)
    qn_t = _projt(cqn, w_qn_t, seq=seq, scale=q_scale, name="projt_qnope")
    qr_t = _projt(cqn, w_qr_t, seq=seq, rope=tabs_mla_t, rope_half=C_ROPE // 2, scale=q_scale, name="projt_qrope")
    kn = _proj(ckvn, w_kn, seq=seq, out_dtype=jnp.bfloat16, name="proj_knope")
    v_t = _projt(ckvn, w_v_t, seq=seq, name="projt_v")
    r3 = lambda a: a.reshape(batch, seq, -1)
    out_c = _mla(qn_t, qr_t, kn, k_rope, v_t, batch=batch, seq=seq)
    out_d = _rglru(r3(po), conv_w, conv_b, w_a, b_a, w_x, b_x, lam, batch=batch, seq=seq).reshape(batch * seq, -1)
    mix = jnp.concatenate([out_c, out_d], axis=-1)
    x = _proj(mix, w_out, seq=seq, res=x, name="proj_out")
    return _moe(x, norm_ffn, w_router, w_gate, w_up, w_down, layer)


def kernel(x, ev_norm_mix, ev_w_in, ev_w_out, ev_norm_ffn, ffn_w_gate, ffn_w_up, ffn_w_down, od_norm_mix, od_w_in, mla_q_norm, mla_w_uq, mla_kv_norm, mla_w_ukv, rg_conv_w, rg_conv_b, rg_w_a, rg_b_a, rg_w_x, rg_b_x, rg_lambda, od_w_out, od_norm_ffn, moe_router, moe_w_gate, moe_w_up, moe_w_down, final_norm):
    batch, seq, d = x.shape
    depth = ev_w_in.shape[0] + od_w_in.shape[0]
    assert d == D_MODEL and seq % KEY_CHUNK == 0 and (batch * seq) % FFN_ROW_TILE == 0
    tabs64 = _rope_tables(seq, HEAD_DIM, ROT_DIM)
    tabs32 = _rope_tables(seq, IDX_DIM, IDX_ROT)
    tabs_mla = _rope_tables(seq, C_ROPE, C_ROPE)
    h = x.reshape(batch * seq, d)
    for layer in range(depth):
        i = layer // 2
        if layer % 2 == 0:
            h = _even_layer(h, batch, seq, i, ev_norm_mix[i], ev_w_in[i], ev_w_out[i], ev_norm_ffn[i],
                            ffn_w_gate, ffn_w_up, ffn_w_down, tabs64, tabs32)
        else:
            h = _odd_layer(h, batch, seq, i, od_norm_mix[i], od_w_in[i], mla_q_norm[i], mla_w_uq[i], mla_kv_norm[i],
                           mla_w_ukv[i], rg_conv_w[i], rg_conv_b[i], rg_w_a[i], rg_b_a[i], rg_w_x[i], rg_b_x[i],
                           rg_lambda[i], od_w_out[i], od_norm_ffn[i], moe_router[i], moe_w_gate, moe_w_up,
                           moe_w_down, tabs_mla)
    out = _rmsnorm(h, final_norm, out_dtype=jnp.float32, name="rmsnorm_final")
    return out.reshape(batch, seq, d)
```

```python
import functools

import jax
import jax.numpy as jnp
import numpy as np
from jax import lax
from jax.experimental import pallas as pl
from jax.experimental.pallas import tpu as pltpu

D_MODEL = 1024
HEAD_DIM = 64
ROT_DIM = HEAD_DIM // 4
ROPE_THETA = 500000.0
NORM_EPS = 1e-6

A_HEADS = 8
A_KV_HEADS = 2
IDX_HEADS = 8
IDX_DIM = 32
IDX_ROT = IDX_DIM // 4
TOPK_MAX = 256

B_HEADS = 8
DILATED_PATTERNS = ((128, 1), (512, 4), (2048, 16))

C_HEADS = 8
C_NOPE = 64
C_ROPE = 32
C_V = 64
Q_LORA = 256
KV_LORA = 128

D_RNN = 512
RG_BLOCKS = 8
RG_BW = D_RNN // RG_BLOCKS
CONV_W = 4
RG_C = 8.0

D_FF = 3584
N_EXPERTS = 8
TOP_K = 2

LANES = 128
VMEM_LIMIT_BYTES = 56 * 1024 * 1024
BLOCK_Q = 128
KEY_CHUNK = 512
MLA_BLOCK_Q = 256
DSA_BLOCK_Q = 256
SUBLANES = 8
LOG2_E = 1.4426950408889634
ROW_TILE = 512
FFN_ROW_TILE = 1024
FFN_COL_TILE = 512
MOE_ROW_TILE = 1024
GATHER_ROWS = 256
DISPATCH_ROWS = 512
RG_CHUNK = 256
NEG_BIG = -1e30
INT_MIN = -2147483648

_NT = (((1,), (1,)), ((), ()))


def _params(semantics, vmem=VMEM_LIMIT_BYTES):
    return pltpu.CompilerParams(dimension_semantics=semantics, vmem_limit_bytes=vmem)


def _bf16(a):
    return a if a.dtype == jnp.bfloat16 else a.astype(jnp.bfloat16)


def _rmsnorm_kernel(x_ref, g_ref, o_ref):
    x = x_ref[...].astype(jnp.float32)
    ms = jnp.mean(x * x, axis=-1, keepdims=True)
    o_ref[...] = (x * lax.rsqrt(ms + NORM_EPS) * g_ref[...]).astype(o_ref.dtype)


def _rmsnorm(x, g, *, width=None, col_block=0, out_dtype=jnp.bfloat16, name="rmsnorm"):
    n = x.shape[0]
    width = x.shape[1] if width is None else width
    return pl.pallas_call(
        _rmsnorm_kernel,
        grid=(n // ROW_TILE,),
        in_specs=[pl.BlockSpec((ROW_TILE, width), lambda i: (i, col_block)),
                  pl.BlockSpec((1, width), lambda i: (0, 0))],
        out_specs=pl.BlockSpec((ROW_TILE, width), lambda i: (i, 0)),
        out_shape=jax.ShapeDtypeStruct((n, width), out_dtype),
        compiler_params=_params(("parallel",)),
        name=name,
    )(x, g.reshape(1, width).astype(jnp.float32))


def _rope_tables(seq, period, rot_dim):
    half = rot_dim // 2
    inv_freq = 1.0 / (ROPE_THETA ** (jnp.arange(half, dtype=jnp.float32) * (2.0 / rot_dim)))
    ang = jnp.arange(seq, dtype=jnp.float32)[:, None] * inv_freq[None, :]
    cos, sin = jnp.cos(ang), jnp.sin(ang)
    rest = period - rot_dim
    ones = jnp.ones((seq, rest), jnp.float32)
    zeros = jnp.zeros((seq, rest), jnp.float32)
    zh = jnp.zeros((seq, half), jnp.float32)
    c = jnp.concatenate([cos, cos, ones], axis=1)
    sa = jnp.concatenate([-sin, zh, zeros], axis=1)
    sb = jnp.concatenate([zh, sin, zeros], axis=1)
    reps = LANES // period
    return tuple(jnp.tile(t, (1, reps)) for t in (c, sa, sb))


def _proj_kernel(*refs, rope_half, has_res):
    a_ref, w_ref = refs[0], refs[1]
    o_ref = refs[-1]
    y = jnp.dot(_bf16(a_ref[...]), _bf16(w_ref[...]), preferred_element_type=jnp.float32)
    pos = 2
    if rope_half:
        c_ref, sa_ref, sb_ref = refs[2:5]
        pos = 5
        tn = y.shape[1]
        reps = tn // LANES
        c = jnp.tile(c_ref[...], (1, reps))
        sa = jnp.tile(sa_ref[...], (1, reps))
        sb = jnp.tile(sb_ref[...], (1, reps))
        y = y * c + pltpu.roll(y, tn - rope_half, 1) * sa + pltpu.roll(y, rope_half, 1) * sb
    if has_res:
        y = y + refs[pos][...]
    o_ref[...] = y.astype(o_ref.dtype)


def _proj(a, w, *, seq, rope=None, rope_half=0, res=None, out_dtype=jnp.float32, name="proj"):
    n, k = a.shape
    np_ = w.shape[1]
    tm = ROW_TILE
    seq_blocks = seq // tm
    in_specs = [pl.BlockSpec((tm, k), lambda i: (i, 0)),
                pl.BlockSpec((k, np_), lambda i: (0, 0))]
    args = [a, w]
    if rope is not None:
        for t in rope:
            in_specs.append(pl.BlockSpec((tm, LANES), lambda i: (i % seq_blocks, 0)))
            args.append(t)
    if res is not None:
        in_specs.append(pl.BlockSpec((tm, np_), lambda i: (i, 0)))
        args.append(res)
    return pl.pallas_call(
        functools.partial(_proj_kernel, rope_half=rope_half if rope is not None else 0, has_res=res is not None),
        grid=(n // tm,),
        in_specs=in_specs,
        out_specs=pl.BlockSpec((tm, np_), lambda i: (i, 0)),
        out_shape=jax.ShapeDtypeStruct((n, np_), out_dtype),
        compiler_params=_params(("parallel",)),
        name=name,
    )(*args)


def _out_proj_kernel(a1_ref, a2_ref, w1_ref, w2_ref, res_ref, o_ref):
    y = jnp.dot(_bf16(a1_ref[...]), _bf16(w1_ref[...]), preferred_element_type=jnp.float32)
    y = y + jnp.dot(_bf16(a2_ref[...]), _bf16(w2_ref[...]), preferred_element_type=jnp.float32)
    o_ref[...] = res_ref[...] + y


def _out_proj(a1, a2, w, res):
    n, k1 = a1.shape
    k2 = a2.shape[1]
    d = w.shape[1]
    tm = ROW_TILE
    return pl.pallas_call(
        _out_proj_kernel,
        grid=(n // tm,),
        in_specs=[pl.BlockSpec((tm, k1), lambda i: (i, 0)),
                  pl.BlockSpec((tm, k2), lambda i: (i, 0)),
                  pl.BlockSpec((k1, d), lambda i: (0, 0)),
                  pl.BlockSpec((k2, d), lambda i: (0, 0)),
                  pl.BlockSpec((tm, d), lambda i: (i, 0))],
        out_specs=pl.BlockSpec((tm, d), lambda i: (i, 0)),
        out_shape=jax.ShapeDtypeStruct((n, d), jnp.float32),
        compiler_params=_params(("parallel",)),
        name="proj_out",
    )(a1, a2, w[:k1], w[k1:], res)


def _projt_kernel(*refs, rope_half, scale):
    wt_ref, a_ref = refs[0], refs[1]
    o_ref = refs[-1]
    y = lax.dot_general(_bf16(wt_ref[...]), _bf16(a_ref[...]), _NT, preferred_element_type=jnp.float32)
    if rope_half:
        c_ref, sa_ref, sb_ref = refs[2:5]
        rows = y.shape[0]
        reps = rows // LANES
        c = jnp.tile(c_ref[...], (reps, 1))
        sa = jnp.tile(sa_ref[...], (reps, 1))
        sb = jnp.tile(sb_ref[...], (reps, 1))
        y = y * c + pltpu.roll(y, rows - rope_half, 0) * sa + pltpu.roll(y, rope_half, 0) * sb
    if scale != 1.0:
        y = y * scale
    o_ref[0] = y.astype(o_ref.dtype)


def _projt(a, wt, *, seq, rope=None, rope_half=0, scale=1.0, out_dtype=jnp.bfloat16, name="projt"):
    n, k = a.shape
    np_ = wt.shape[0]
    tm = ROW_TILE
    seq_blocks = seq // tm
    in_specs = [pl.BlockSpec((np_, k), lambda i: (0, 0)),
                pl.BlockSpec((tm, k), lambda i: (i, 0))]
    args = [wt, a]
    if rope is not None:
        for t in rope:
            in_specs.append(pl.BlockSpec((LANES, tm), lambda i: (0, i % seq_blocks)))
            args.append(t)
    return pl.pallas_call(
        functools.partial(_projt_kernel, rope_half=rope_half if rope is not None else 0, scale=scale),
        grid=(n // tm,),
        in_specs=in_specs,
        out_specs=pl.BlockSpec((1, np_, tm), lambda i: (i, 0, 0)),
        out_shape=jax.ShapeDtypeStruct((n // tm, np_, tm), out_dtype),
        compiler_params=_params(("parallel",)),
        name=name,
    )(*args)


def _silu(g):
    return g * (1.0 / (1.0 + jnp.exp(-g)))


def _ffn_kernel(xn_ref, wg_ref, wu_ref, wd_ref, res_ref, o_ref, acc_ref):
    f = pl.program_id(1)

    @pl.when(f == 0)
    def _():
        acc_ref[...] = jnp.zeros_like(acc_ref)

    xn = xn_ref[...]
    g = jnp.dot(xn, _bf16(wg_ref[0]), preferred_element_type=jnp.float32)
    u = jnp.dot(xn, _bf16(wu_ref[0]), preferred_element_type=jnp.float32)
    h = (_silu(g) * u).astype(jnp.bfloat16)
    acc_ref[...] += jnp.dot(h, _bf16(wd_ref[0]), preferred_element_type=jnp.float32)

    @pl.when(f == pl.num_programs(1) - 1)
    def _():
        o_ref[...] = res_ref[...] + acc_ref[...]


def _ffn(xn, w_gate, w_up, w_down, res, layer):
    n, d = xn.shape
    ff = w_gate.shape[2]
    tm, tf = FFN_ROW_TILE, FFN_COL_TILE
    return pl.pallas_call(
        _ffn_kernel,
        grid=(n // tm, ff // tf),
        in_specs=[pl.BlockSpec((tm, d), lambda i, f: (i, 0)),
                  pl.BlockSpec((1, d, tf), lambda i, f: (layer, 0, f)),
                  pl.BlockSpec((1, d, tf), lambda i, f: (layer, 0, f)),
                  pl.BlockSpec((1, tf, d), lambda i, f: (layer, f, 0)),
                  pl.BlockSpec((tm, d), lambda i, f: (i, 0))],
        out_specs=pl.BlockSpec((tm, d), lambda i, f: (i, 0)),
        out_shape=jax.ShapeDtypeStruct((n, d), jnp.float32),
        scratch_shapes=[pltpu.VMEM((tm, d), jnp.float32)],
        compiler_params=_params(("parallel", "arbitrary")),
        name="ffn_swiglu",
    )(xn, w_gate, w_up, w_down, res)


def _dsa_kernel(qt_ref, k_ref, vt_ref, qit_ref, ki_ref, wt_ref, o_ref, key_scr, bias_scr, *scr, topk):
    acc_scr, m_scr, l_scr = scr[:A_HEADS], scr[A_HEADS:2 * A_HEADS], scr[2 * A_HEADS:]
    i = pl.program_id(1)
    tq = DSA_BLOCK_Q
    n_chunks = (i * tq + tq + KEY_CHUNK - 1) // KEY_CHUNK
    qpos = i * tq + lax.broadcasted_iota(jnp.int32, (1, tq), 1)
    kloc = lax.broadcasted_iota(jnp.int32, (KEY_CHUNK, 1), 0)

    w = wt_ref[0]

    def index_body(c, carry):
        kc = ki_ref[pl.ds(c * KEY_CHUNK, KEY_CHUNK), :][:, :IDX_DIM]
        score = jnp.zeros((KEY_CHUNK, tq), jnp.float32)
        for h in range(IDX_HEADS):
            lg = jnp.dot(kc, qit_ref[0, h * IDX_DIM:(h + 1) * IDX_DIM, :], preferred_element_type=jnp.float32)
            score = score + w[h:h + 1, :] * jnp.maximum(lg, 0.0)
        bits = pltpu.bitcast(score, jnp.int32)
        key = jnp.where(bits < 0, bits ^ jnp.int32(0x7FFFFFFF), bits)
        key_scr[c] = jnp.where(c * KEY_CHUNK + kloc <= qpos, key, jnp.int32(INT_MIN))
        return carry

    lax.fori_loop(0, n_chunks, index_body, 0)

    def count(pred):
        def body(c, acc):
            hit = pred(key_scr[c]).astype(jnp.int32)
            return acc + jnp.sum(hit.reshape(KEY_CHUNK // SUBLANES, SUBLANES, tq), axis=0)
        acc = lax.fori_loop(0, n_chunks, body, jnp.zeros((SUBLANES, tq), jnp.int32))
        return jnp.sum(acc, axis=0, keepdims=True)

    def bit_body(t, res):
        cand = res | jnp.left_shift(jnp.int32(1), 31 - t)
        cand_key = cand ^ jnp.int32(INT_MIN)
        return jnp.where(count(lambda key: key >= cand_key) >= topk, cand, res)

    thr = lax.fori_loop(0, 32, bit_body, jnp.zeros((1, tq), jnp.int32)) ^ jnp.int32(INT_MIN)
    need = (topk - count(lambda key: key > thr)).astype(jnp.float32)

    tri = (lax.broadcasted_iota(jnp.int32, (KEY_CHUNK, KEY_CHUNK), 1)
           <= lax.broadcasted_iota(jnp.int32, (KEY_CHUNK, KEY_CHUNK), 0)).astype(jnp.bfloat16)

    def mask_body(c, seen):
        key = key_scr[c]
        eq = key == thr
        running = jnp.dot(tri, eq.astype(jnp.bfloat16), preferred_element_type=jnp.float32) + seen
        sel = ((key > thr) | (eq & (running <= need))) & (c * KEY_CHUNK + kloc <= qpos)
        bias_scr[c] = jnp.where(sel, 0.0, NEG_BIG)
        return running[KEY_CHUNK - 1:KEY_CHUNK, :]

    lax.fori_loop(0, n_chunks, mask_body, jnp.zeros((1, tq), jnp.float32))

    for h in range(A_HEADS):
        acc_scr[h][...] = jnp.zeros_like(acc_scr[h])
        l_scr[h][...] = jnp.zeros_like(l_scr[h])
        m_scr[h][...] = jnp.full_like(m_scr[h], NEG_BIG)
    group = A_HEADS // A_KV_HEADS

    def att_body(c, carry):
        rows = pl.ds(c * KEY_CHUNK, KEY_CHUNK)
        bias = bias_scr[c]
        scores = []
        for h in range(A_HEADS):
            n = h // group
            scores.append(jnp.dot(k_ref[rows, n * HEAD_DIM:(n + 1) * HEAD_DIM],
                                  qt_ref[0, h * HEAD_DIM:(h + 1) * HEAD_DIM, :],
                                  preferred_element_type=jnp.float32))
        probs, alphas = [], []
        for h in range(A_HEADS):
            s = scores[h] + bias
            m_old = m_scr[h][...]
            m_new = jnp.maximum(m_old, jnp.max(s, axis=0, keepdims=True))
            alpha = jnp.exp2(m_old - m_new)
            p = jnp.exp2(s - m_new)
            l_scr[h][...] = l_scr[h][...] * alpha + jnp.sum(p, axis=0, keepdims=True)
            m_scr[h][...] = m_new
            probs.append(p.astype(jnp.bfloat16))
            alphas.append(alpha)
        for h in range(A_HEADS):
            n = h // group
            acc_scr[h][...] = acc_scr[h][...] * alphas[h] + jnp.dot(
                vt_ref[c, n * HEAD_DIM:(n + 1) * HEAD_DIM, :], probs[h], preferred_element_type=jnp.float32)
        return carry

    lax.fori_loop(0, n_chunks, att_body, 0)
    out_t = jnp.concatenate([acc_scr[h][...] * (1.0 / l_scr[h][...]) for h in range(A_HEADS)], axis=0)
    o_ref[...] = out_t.T


def _dsa(qa_t, ka, va_t, qi_t, ki, w_t, *, batch, seq):
    topk = min(TOPK_MAX, seq // 4)
    tq = DSA_BLOCK_Q
    tiles = seq // ROW_TILE
    per_tile = ROW_TILE // tq
    n_chunks = seq // KEY_CHUNK
    qw = A_HEADS * HEAD_DIM
    kvw = A_KV_HEADS * HEAD_DIM
    q_tile = lambda b, i: (b * tiles + i // per_tile, 0, i % per_tile)
    return pl.pallas_call(
        functools.partial(_dsa_kernel, topk=topk),
        grid=(batch, seq // tq),
        in_specs=[pl.BlockSpec((1, qw, tq), q_tile),
                  pl.BlockSpec((seq, kvw), lambda b, i: (b, 0)),
                  pl.BlockSpec((tiles, kvw, ROW_TILE), lambda b, i: (b, 0, 0)),
                  pl.BlockSpec((1, IDX_HEADS * IDX_DIM, tq), q_tile),
                  pl.BlockSpec((seq, LANES), lambda b, i: (b, 0)),
                  pl.BlockSpec((1, IDX_HEADS, tq), q_tile)],
        out_specs=pl.BlockSpec((tq, qw), lambda b, i: (b * (seq // tq) + i, 0)),
        out_shape=jax.ShapeDtypeStruct((batch * seq, qw), jnp.float32),
        scratch_shapes=([pltpu.VMEM((n_chunks, KEY_CHUNK, tq), jnp.int32),
                         pltpu.VMEM((n_chunks, KEY_CHUNK, tq), jnp.float32)]
                        + [pltpu.VMEM((HEAD_DIM, tq), jnp.float32)] * A_HEADS
                        + [pltpu.VMEM((1, tq), jnp.float32)] * (2 * A_HEADS)),
        compiler_params=_params(("parallel", "arbitrary")),
        name="dsa_attention",
    )(qa_t, ka, va_t, qi_t, ki, w_t)


def _dilated_kernel(*refs, seq):
    pairs = B_HEADS // 2
    q_refs, k_refs, v_refs = refs[:pairs], refs[pairs:2 * pairs], refs[2 * pairs:3 * pairs]
    o_ref = refs[3 * pairs]
    scr = refs[3 * pairs + 1:]
    acc_scr, m_scr, l_scr = scr[:pairs], scr[pairs:2 * pairs], scr[2 * pairs:]
    scale = HEAD_DIM ** -0.5
    row = lax.broadcasted_iota(jnp.int32, (BLOCK_Q, BLOCK_Q), 0)
    col = lax.broadcasted_iota(jnp.int32, (BLOCK_Q, BLOCK_Q), 1)
    mask_cur = col <= row
    mask_prev = col >= row

    for p in range(pairs):
        acc_scr[p][...] = jnp.zeros_like(acc_scr[p])
        l_scr[p][...] = jnp.zeros_like(l_scr[p])
        m_scr[p][...] = jnp.full_like(m_scr[p], NEG_BIG)

    def gather(ref_list, rows):
        return jnp.concatenate([_bf16(r[0, rows, :]) for r in ref_list], axis=1)

    def spread(cols):
        return jnp.concatenate([jnp.broadcast_to(c, (BLOCK_Q, HEAD_DIM)) for c in cols], axis=1)

    for (window, dil) in DILATED_PATTERNS:
        assert window // dil == BLOCK_Q and seq % (dil * BLOCK_Q) == 0
        n_blocks = seq // (dil * BLOCK_Q)

        def block_body(idx, carry, dil=dil, n_blocks=n_blocks):
            phase = idx // n_blocks
            blk = idx % n_blocks
            def view_rows(block):
                if dil == 1:
                    return pl.ds(pl.multiple_of(BLOCK_Q * block, BLOCK_Q), BLOCK_Q)
                return pl.ds(phase + dil * BLOCK_Q * block, BLOCK_Q, stride=dil)

            rows = view_rows(blk)
            q2 = gather(q_refs, rows)
            k_cur = gather(k_refs, rows)
            v_cur = gather(v_refs, rows)
            if n_blocks > 1:
                rows_p = view_rows(jnp.maximum(blk - 1, 0))
                k_prev = gather(k_refs, rows_p)
                v_prev = gather(v_refs, rows_p)
                mask_p = mask_prev & (blk > 0)
            m_old = jnp.concatenate([m[rows, :] for m in m_scr], axis=1)
            heads = q2.shape[1] // HEAD_DIM
            lanes = [slice(h * HEAD_DIM, (h + 1) * HEAD_DIM) for h in range(heads)]
            s_cur = [lax.dot_general(q2[:, sl], k_cur[:, sl], _NT, preferred_element_type=jnp.float32) for sl in lanes]
            if n_blocks > 1:
                s_prev = [lax.dot_general(q2[:, sl], k_prev[:, sl], _NT, preferred_element_type=jnp.float32)
                          for sl in lanes]
            new_m, new_sum, p_cur, p_prev = [], [], [], []
            for h in range(heads):
                s_c = jnp.where(mask_cur, s_cur[h] * scale, NEG_BIG)
                mx = jnp.max(s_c, axis=1, keepdims=True)
                if n_blocks > 1:
                    s_p = jnp.where(mask_p, s_prev[h] * scale, NEG_BIG)
                    mx = jnp.maximum(mx, jnp.max(s_p, axis=1, keepdims=True))
                m_h = jnp.maximum(m_old[:, h * HEAD_DIM:h * HEAD_DIM + 1], mx)
                p_c = jnp.exp(s_c - m_h)
                psum = jnp.sum(p_c, axis=1, keepdims=True)
                p_cur.append(p_c.astype(jnp.bfloat16))
                if n_blocks > 1:
                    p_p = jnp.exp(s_p - m_h)
                    psum = psum + jnp.sum(p_p, axis=1, keepdims=True)
                    p_prev.append(p_p.astype(jnp.bfloat16))
                new_m.append(m_h)
                new_sum.append(psum)
            new_pv = []
            for h in range(heads):
                pv = jnp.dot(p_cur[h], v_cur[:, lanes[h]], preferred_element_type=jnp.float32)
                if n_blocks > 1:
                    pv = pv + jnp.dot(p_prev[h], v_prev[:, lanes[h]], preferred_element_type=jnp.float32)
                new_pv.append(pv)
            m_new = spread(new_m)
            alpha = jnp.exp(m_old - m_new)
            sums = spread(new_sum)
            pvs = jnp.concatenate(new_pv, axis=1)
            for p in range(pairs):
                cols = slice(p * LANES, (p + 1) * LANES)
                m_scr[p][rows, :] = m_new[:, cols]
                l_scr[p][rows, :] = l_scr[p][rows, :] * alpha[:, cols] + sums[:, cols]
                acc_scr[p][rows, :] = acc_scr[p][rows, :] * alpha[:, cols] + pvs[:, cols]
            return carry

        lax.fori_loop(0, dil * n_blocks, block_body, 0)

    for p in range(pairs):
        o_ref[0, :, p * LANES:(p + 1) * LANES] = acc_scr[p][...] * (1.0 / l_scr[p][...])


def _dilated(qk, v, *, batch, seq):
    width = B_HEADS * HEAD_DIM
    pairs = B_HEADS // 2

    def col_block(j):
        return pl.BlockSpec((1, seq, LANES), lambda b: (b, 0, j))

    return pl.pallas_call(
        functools.partial(_dilated_kernel, seq=seq),
        grid=(batch,),
        in_specs=([col_block(p) for p in range(pairs)] + [col_block(pairs + p) for p in range(pairs)]
                  + [col_block(p) for p in range(pairs)]),
        out_specs=pl.BlockSpec((1, seq, width), lambda b: (b, 0, 0)),
        out_shape=jax.ShapeDtypeStruct((batch, seq, width), jnp.float32),
        scratch_shapes=[pltpu.VMEM((seq, LANES), jnp.float32)] * (3 * pairs),
        compiler_params=_params(("parallel",)),
        name="dilated_attention",
    )(*([qk] * (2 * pairs) + [v] * pairs))


def _mla_kernel(qn_ref, qr_ref, kn_ref, kr_ref, vt_ref, o_ref, *scr):
    acc_scr, m_scr, l_scr = scr[:C_HEADS], scr[C_HEADS:2 * C_HEADS], scr[2 * C_HEADS:]
    i = pl.program_id(1)
    tq = MLA_BLOCK_Q
    n_full = (i * tq + 1) // KEY_CHUNK
    n_all = (i * tq + tq + KEY_CHUNK - 1) // KEY_CHUNK
    qpos = i * tq + lax.broadcasted_iota(jnp.int32, (1, tq), 1)
    kloc = lax.broadcasted_iota(jnp.int32, (KEY_CHUNK, 1), 0)

    for h in range(C_HEADS):
        acc_scr[h][...] = jnp.zeros_like(acc_scr[h])
        l_scr[h][...] = jnp.zeros_like(l_scr[h])
        m_scr[h][...] = jnp.full_like(m_scr[h], NEG_BIG)

    def chunk(c, carry, masked):
        rows = pl.ds(c * KEY_CHUNK, KEY_CHUNK)
        kr = kr_ref[rows, :][:, :C_ROPE]
        visible = (c * KEY_CHUNK + kloc) <= qpos
        scores = []
        for h in range(C_HEADS):
            hn = slice(h * C_NOPE, (h + 1) * C_NOPE)
            scores.append(jnp.dot(kn_ref[rows, hn], qn_ref[0, hn, :], preferred_element_type=jnp.float32)
                          + jnp.dot(kr, qr_ref[0, h * C_ROPE:(h + 1) * C_ROPE, :],
                                    preferred_element_type=jnp.float32))
        probs, alphas = [], []
        for h in range(C_HEADS):
            s = jnp.where(visible, scores[h], NEG_BIG) if masked else scores[h]
            m_old = m_scr[h][...]
            m_new = jnp.maximum(m_old, jnp.max(s, axis=0, keepdims=True))
            alpha = jnp.exp2(m_old - m_new)
            p = jnp.exp2(s - m_new)
            l_scr[h][...] = l_scr[h][...] * alpha + jnp.sum(p, axis=0, keepdims=True)
            m_scr[h][...] = m_new
            probs.append(p.astype(jnp.bfloat16))
            alphas.append(alpha)
        for h in range(C_HEADS):
            hv = slice(h * C_V, (h + 1) * C_V)
            acc_scr[h][...] = acc_scr[h][...] * alphas[h] + jnp.dot(vt_ref[c, hv, :], probs[h],
                                                                     preferred_element_type=jnp.float32)
        return carry

    lax.fori_loop(0, n_full, functools.partial(chunk, masked=False), 0)
    lax.fori_loop(n_full, n_all, functools.partial(chunk, masked=True), 0)

    out_t = jnp.concatenate([acc_scr[h][...] * (1.0 / l_scr[h][...]) for h in range(C_HEADS)], axis=0)
    o_ref[...] = out_t.T


def _mla(qn_t, qr_t, kn, kr, v_t, *, batch, seq):
    tq = MLA_BLOCK_Q
    tiles = seq // ROW_TILE
    per_tile = ROW_TILE // tq
    kw = C_HEADS * C_NOPE
    vw = C_HEADS * C_V
    n = batch * seq
    return pl.pallas_call(
        _mla_kernel,
        grid=(batch, seq // tq),
        in_specs=[pl.BlockSpec((1, kw, tq), lambda b, i: (b * tiles + i // per_tile, 0, i % per_tile)),
                  pl.BlockSpec((1, C_HEADS * C_ROPE, tq), lambda b, i: (b * tiles + i // per_tile, 0, i % per_tile)),
                  pl.BlockSpec((seq, kw), lambda b, i: (b, 0)),
                  pl.BlockSpec((seq, LANES), lambda b, i: (b, 0)),
                  pl.BlockSpec((tiles, vw, ROW_TILE), lambda b, i: (b, 0, 0))],
        out_specs=pl.BlockSpec((tq, vw), lambda b, i: (b * (seq // tq) + i, 0)),
        out_shape=jax.ShapeDtypeStruct((n, vw), jnp.float32),
        scratch_shapes=([pltpu.VMEM((C_V, tq), jnp.float32)] * C_HEADS
                        + [pltpu.VMEM((1, tq), jnp.float32)] * (2 * C_HEADS)),
        compiler_params=_params(("parallel", "arbitrary")),
        name="mla_attention",
    )(qn_t, qr_t, kn, kr, v_t)


def _expm1(y):
    u = jnp.exp(y)
    safe = jnp.where(u == 1.0, 2.0, u)
    return jnp.where(u == 1.0, y, (u - 1.0) * y / jnp.log(safe))


def _gelu_tanh(x):
    return 0.5 * x * (1.0 + jnp.tanh(np.sqrt(2.0 / np.pi).astype(np.float32) * (x + 0.044715 * (x * x * x))))


def _rglru_kernel(x_ref, g_ref, cw_ref, cb_ref, wa_ref, ba_ref, wx_ref, bx_ref, lam_ref, o_ref, a_scr, u_scr, *, seq):
    cw = cw_ref[...]
    neg_c_softplus = -RG_C * jnp.log1p(jnp.exp(-jnp.abs(-lam_ref[...]))) - RG_C * jnp.maximum(-lam_ref[...], 0.0)
    hi = lax.Precision.HIGHEST
    sub = 8

    def chunk_body(ci, h):
        r0 = pl.multiple_of(ci * RG_CHUNK, RG_CHUNK)
        xa = x_ref[0, pl.ds(r0, RG_CHUNK), :]
        prev = x_ref[0, pl.ds(pl.multiple_of(jnp.maximum(r0 - sub, 0), sub), sub), :]
        prev = jnp.where(ci > 0, prev, 0.0)
        xcat = jnp.concatenate([prev, xa], axis=0)
        xc = cw[CONV_W - 1:CONV_W, :] * xa + cb_ref[...]
        for j in range(1, CONV_W):
            xc = xc + cw[CONV_W - 1 - j:CONV_W - j, :] * pltpu.roll(xcat, j, 0)[sub:, :]
        r = 1.0 / (1.0 + jnp.exp(-(jnp.dot(xc, wa_ref[...], precision=hi, preferred_element_type=jnp.float32)
                                   + ba_ref[...])))
        ig = 1.0 / (1.0 + jnp.exp(-(jnp.dot(xc, wx_ref[...], precision=hi, preferred_element_type=jnp.float32)
                                    + bx_ref[...])))
        log_a = neg_c_softplus * r
        a_scr[...] = jnp.exp(log_a)
        u_scr[...] = jnp.sqrt(-_expm1(2.0 * log_a)) * (ig * xc)
        gate = _gelu_tanh(g_ref[0, pl.ds(r0, RG_CHUNK), :])

        def group_body(gi, h):
            g0 = pl.multiple_of(gi * sub, sub)
            a8 = a_scr[pl.ds(g0, sub), :]
            u8 = u_scr[pl.ds(g0, sub), :]
            rows = []
            for t in range(sub):
                h = a8[t:t + 1, :] * h + u8[t:t + 1, :]
                rows.append(h)
            u_scr[pl.ds(g0, sub), :] = jnp.concatenate(rows, axis=0)
            return h

        h = lax.fori_loop(0, RG_CHUNK // sub, group_body, h)
        o_ref[0, pl.ds(r0, RG_CHUNK), :] = u_scr[...] * gate
        return h

    lax.fori_loop(0, seq // RG_CHUNK, chunk_body, jnp.zeros((1, D_RNN), jnp.float32))


def _block_diag(w):
    nb, bw, _ = w.shape
    eye = jnp.eye(nb, dtype=w.dtype)
    return (eye[:, None, :, None] * w[:, :, None, :]).reshape(nb * bw, nb * bw)


def _rglru(po, conv_w, conv_b, w_a, b_a, w_x, b_x, lam, *, batch, seq):
    row = lambda a: a.reshape(1, D_RNN).astype(jnp.float32)
    vec = pl.BlockSpec((1, D_RNN), lambda b: (0, 0))
    mat = pl.BlockSpec((D_RNN, D_RNN), lambda b: (0, 0))
    return pl.pallas_call(
        functools.partial(_rglru_kernel, seq=seq),
        grid=(batch,),
        in_specs=[pl.BlockSpec((1, seq, D_RNN), lambda b: (b, 0, 0)),
                  pl.BlockSpec((1, seq, D_RNN), lambda b: (b, 0, 1)),
                  pl.BlockSpec((CONV_W, D_RNN), lambda b: (0, 0)),
                  vec, mat, vec, mat, vec, vec],
        out_specs=pl.BlockSpec((1, seq, D_RNN), lambda b: (b, 0, 0)),
        out_shape=jax.ShapeDtypeStruct((batch, seq, D_RNN), jnp.float32),
        scratch_shapes=[pltpu.VMEM((RG_CHUNK, D_RNN), jnp.float32)] * 2,
        compiler_params=_params(("parallel",)),
        name="rglru",
    )(po, po, conv_w.astype(jnp.float32), row(conv_b), _block_diag(w_a), row(b_a), _block_diag(w_x), row(b_x), row(lam))


def _pack_bf16_pairs(x):
    w = x.shape[1] // 2
    hi = pltpu.bitcast(x[:, :w].astype(jnp.bfloat16).astype(jnp.float32), jnp.uint32)
    lo = pltpu.bitcast(x[:, w:].astype(jnp.bfloat16).astype(jnp.float32), jnp.uint32)
    return hi | (lo >> 16)


def _unpack_bf16_pairs(p):
    hi = pltpu.bitcast(p & jnp.uint32(0xFFFF0000), jnp.float32)
    lo = pltpu.bitcast(p << 16, jnp.float32)
    return jnp.concatenate([hi, lo], axis=1).astype(jnp.bfloat16)


def _router_kernel(x_ref, g_ref, wr_ref, xn_ref, idx_ref, gate_ref):
    x = x_ref[...]
    ms = jnp.mean(x * x, axis=-1, keepdims=True)
    xn = x * lax.rsqrt(ms + NORM_EPS) * g_ref[...]
    xn_ref[...] = _pack_bf16_pairs(xn)
    logits = jnp.dot(xn, wr_ref[...], precision=lax.Precision.HIGHEST, preferred_element_type=jnp.float32)
    lane = lax.broadcasted_iota(jnp.int32, logits.shape, 1)
    logits = jnp.where(lane < N_EXPERTS, logits, -jnp.inf)
    m1 = jnp.max(logits, axis=1, keepdims=True)
    i1 = jnp.min(jnp.where(logits == m1, lane, LANES), axis=1, keepdims=True)
    rest = jnp.where(lane == i1, -jnp.inf, logits)
    m2 = jnp.max(rest, axis=1, keepdims=True)
    i2 = jnp.min(jnp.where(rest == m2, lane, LANES), axis=1, keepdims=True)
    e = jnp.exp(m2 - m1)
    g1 = 1.0 / (1.0 + e)
    g2 = e / (1.0 + e)
    idx_ref[...] = jnp.where(lane == 0, i1, jnp.where(lane == 1, i2, 0))
    gate_ref[...] = jnp.where(lane == 0, g1, jnp.where(lane == 1, g2, 0.0))


def _router(x, g, w_router):
    n, d = x.shape
    wr = jnp.pad(w_router.astype(jnp.float32), ((0, 0), (0, LANES - N_EXPERTS)))
    tm = ROW_TILE
    return pl.pallas_call(
        _router_kernel,
        grid=(n // tm,),
        in_specs=[pl.BlockSpec((tm, d), lambda i: (i, 0)),
                  pl.BlockSpec((1, d), lambda i: (0, 0)),
                  pl.BlockSpec((d, LANES), lambda i: (0, 0))],
        out_specs=[pl.BlockSpec((tm, d // 2), lambda i: (i, 0)),
                   pl.BlockSpec((tm, LANES), lambda i: (i, 0)),
                   pl.BlockSpec((tm, LANES), lambda i: (i, 0))],
        out_shape=[jax.ShapeDtypeStruct((n, d // 2), jnp.uint32),
                   jax.ShapeDtypeStruct((n, LANES), jnp.int32),
                   jax.ShapeDtypeStruct((n, LANES), jnp.float32)],
        compiler_params=_params(("parallel",)),
        name="moe_router",
    )(x, g.reshape(1, d).astype(jnp.float32), wr)


def _row_copy(src_hbm, dst, src_row, dst_row, sem):
    return pltpu.make_async_copy(src_hbm.at[pl.ds(src_row, 1)], dst.at[pl.ds(dst_row, 1)], sem)


def _gather_kernel(tok_ref, x_hbm, o_ref, x_vmem, sem):
    @pl.when(pl.program_id(0) == 0)
    def _():
        copy = pltpu.make_async_copy(x_hbm, x_vmem, sem)
        copy.start()
        copy.wait()

    base = pl.program_id(0) * DISPATCH_ROWS

    def group(g, carry):
        r0 = pl.multiple_of(g * SUBLANES, SUBLANES)
        rows = [x_vmem[tok_ref[base + r0 + j]] for j in range(SUBLANES)]
        o_ref[pl.ds(r0, SUBLANES), :] = jnp.concatenate(rows, axis=0)
        return carry

    lax.fori_loop(0, DISPATCH_ROWS // SUBLANES, group, 0)


def _gather_rows(row_tok, x):
    n_rows = row_tok.shape[0]
    n, w = x.shape
    return pl.pallas_call(
        _gather_kernel,
        grid_spec=pltpu.PrefetchScalarGridSpec(
            num_scalar_prefetch=1,
            grid=(n_rows // DISPATCH_ROWS,),
            in_specs=[pl.BlockSpec(memory_space=pl.ANY)],
            out_specs=pl.BlockSpec((DISPATCH_ROWS, w), lambda i, tok: (i, 0)),
            scratch_shapes=[pltpu.VMEM((n, 1, w), x.dtype), pltpu.SemaphoreType.DMA(())]),
        out_shape=jax.ShapeDtypeStruct((n_rows, w), x.dtype),
        compiler_params=_params(("arbitrary",)),
        name="moe_gather",
    )(row_tok, x.reshape(n, 1, w))


def _experts_kernel(be_ref, nv_ref, x_ref, wg_ref, wu_ref, wd_ref, o_ref, acc_ref, xb_ref):
    i = pl.program_id(0)
    f = pl.program_id(1)
    last = pl.num_programs(1) - 1

    @pl.when(i < nv_ref[0])
    def _():
        @pl.when(f == 0)
        def _():
            acc_ref[...] = jnp.zeros_like(acc_ref)
            xb_ref[...] = _unpack_bf16_pairs(x_ref[...])

        xb = xb_ref[...]
        g = jnp.dot(xb, _bf16(wg_ref[0]), preferred_element_type=jnp.float32)
        u = jnp.dot(xb, _bf16(wu_ref[0]), preferred_element_type=jnp.float32)
        h = (_silu(g) * u).astype(jnp.bfloat16)
        acc_ref[...] += jnp.dot(h, _bf16(wd_ref[0]), preferred_element_type=jnp.float32)

        @pl.when(f == last)
        def _():
            o_ref[...] = acc_ref[...]

    @pl.when((i >= nv_ref[0]) & (f == last))
    def _():
        o_ref[...] = jnp.zeros_like(o_ref)


def _experts(blk_expert, n_valid, x_rows, w_gate, w_up, w_down):
    n_rows, dw = x_rows.shape
    d = 2 * dw
    ff = w_gate.shape[2]
    tm, tf = MOE_ROW_TILE, FFN_COL_TILE
    n_f = ff // tf

    def fcol(i, f, nv):
        return jnp.where(i < nv[0], f, n_f - 1)

    return pl.pallas_call(
        _experts_kernel,
        grid_spec=pltpu.PrefetchScalarGridSpec(
            num_scalar_prefetch=2,
            grid=(n_rows // tm, n_f),
            in_specs=[pl.BlockSpec((tm, dw), lambda i, f, be, nv: (i, 0)),
                      pl.BlockSpec((1, d, tf), lambda i, f, be, nv: (be[i], 0, fcol(i, f, nv))),
                      pl.BlockSpec((1, d, tf), lambda i, f, be, nv: (be[i], 0, fcol(i, f, nv))),
                      pl.BlockSpec((1, tf, d), lambda i, f, be, nv: (be[i], fcol(i, f, nv), 0))],
            out_specs=pl.BlockSpec((tm, d), lambda i, f, be, nv: (i, 0)),
            scratch_shapes=[pltpu.VMEM((tm, d), jnp.float32), pltpu.VMEM((tm, d), jnp.bfloat16)]),
        out_shape=jax.ShapeDtypeStruct((n_rows, d), jnp.float32),
        compiler_params=_params(("arbitrary", "arbitrary")),
        name="moe_experts",
    )(blk_expert, n_valid, x_rows, w_gate, w_up, w_down)


def _combine_kernel(dest_ref, y_hbm, x_ref, gate_ref, o_ref, buf, sem):
    base = pl.program_id(0) * GATHER_ROWS * TOP_K

    def issue(r, carry):
        for k in range(TOP_K):
            _row_copy(y_hbm, buf.at[k], dest_ref[base + r * TOP_K + k], r, sem).start()
        return carry

    lax.fori_loop(0, GATHER_ROWS, issue, 0)
    for k in range(TOP_K):
        pltpu.make_async_copy(y_hbm.at[pl.ds(0, GATHER_ROWS)], buf.at[k], sem).wait()
    gate = gate_ref[...]
    o_ref[...] = x_ref[...] + gate[:, 0:1] * buf[0] + gate[:, 1:2] * buf[1]


def _combine(dest, y_rows, x, gates):
    n, d = x.shape
    return pl.pallas_call(
        _combine_kernel,
        grid_spec=pltpu.PrefetchScalarGridSpec(
            num_scalar_prefetch=1,
            grid=(n // GATHER_ROWS,),
            in_specs=[pl.BlockSpec(memory_space=pl.ANY),
                      pl.BlockSpec((GATHER_ROWS, d), lambda i, dest: (i, 0)),
                      pl.BlockSpec((GATHER_ROWS, LANES), lambda i, dest: (i, 0))],
            out_specs=pl.BlockSpec((GATHER_ROWS, d), lambda i, dest: (i, 0)),
            scratch_shapes=[pltpu.VMEM((TOP_K, GATHER_ROWS, d), jnp.float32),
                            pltpu.SemaphoreType.DMA(())]),
        out_shape=jax.ShapeDtypeStruct((n, d), jnp.float32),
        compiler_params=_params(("arbitrary",)),
        name="moe_combine",
    )(dest, y_rows, x, gates)


def _moe(x, g, w_router, w_gate, w_up, w_down, layer):
    n = x.shape[0]
    w_gate, w_up, w_down = (w.reshape((-1,) + w.shape[2:]) for w in (w_gate, w_up, w_down))
    xn, idx, gates = _router(x, g, w_router)
    e_flat = idx[:, :TOP_K].reshape(-1)
    onehot = (e_flat[:, None] == jnp.arange(N_EXPERTS, dtype=jnp.int32)[None, :]).astype(jnp.int32)
    counts = jnp.sum(onehot, axis=0)
    rank = jnp.sum((jnp.cumsum(onehot, axis=0) - onehot) * onehot, axis=1)
    padded = (counts + MOE_ROW_TILE - 1) // MOE_ROW_TILE * MOE_ROW_TILE
    pad_end = jnp.cumsum(padded)
    pad_start = pad_end - padded
    dest = (pad_start[e_flat] + rank).astype(jnp.int32)
    n_rows = n * TOP_K + N_EXPERTS * MOE_ROW_TILE
    tok_flat = jnp.repeat(jnp.arange(n, dtype=jnp.int32), TOP_K)
    row_tok = jnp.zeros((n_rows,), jnp.int32).at[dest].set(tok_flat)
    n_blk = n_rows // MOE_ROW_TILE
    blk_expert = jnp.clip(jnp.searchsorted(pad_end, jnp.arange(n_blk, dtype=jnp.int32) * MOE_ROW_TILE, side='right'),
                          0, N_EXPERTS - 1).astype(jnp.int32)
    n_valid = (pad_end[-1] // MOE_ROW_TILE).astype(jnp.int32).reshape(1)
    x_rows = _gather_rows(row_tok, xn)
    y_rows = _experts(blk_expert + layer * N_EXPERTS, n_valid, x_rows, w_gate, w_up, w_down)
    return _combine(dest, y_rows, x, gates)


def _pad_cols(w, width):
    return jnp.pad(w, ((0, 0), (0, width - w.shape[1])))


def _even_layer(x, batch, seq, layer, norm_mix, w_in, w_out, norm_ffn, w_gate, w_up, w_down, tabs64, tabs32):
    qa, ka, va, qi, ki, wi, qb, kb, vb = jnp.split(
        w_in, np.cumsum([512, 128, 128, 256, 32, 8, 512, 512])[:].tolist(), axis=1)
    xn = _rmsnorm(x, norm_mix, name="rmsnorm_mix")
    tabs64_t = tuple(t.T for t in tabs64)
    tabs32_t = tuple(t.T for t in tabs32)
    half64, half32 = ROT_DIM // 2, IDX_ROT // 2
    bf = jnp.bfloat16
    qk_b = _proj(xn, jnp.concatenate([qb, kb], axis=1), seq=seq, rope=tabs64, rope_half=half64, name="proj_qk_b")
    v_b = _proj(xn, vb, seq=seq, name="proj_v_b")
    k_a = _proj(xn, ka, seq=seq, rope=tabs64, rope_half=half64, out_dtype=bf, name="proj_k_a")
    k_i = _proj(xn, _pad_cols(ki, LANES), seq=seq, rope=tabs32, rope_half=half32, out_dtype=bf, name="proj_k_idx")
    qa_t = _projt(xn, qa.T, seq=seq, rope=tabs64_t, rope_half=half64, scale=HEAD_DIM ** -0.5 * LOG2_E, name="projt_q_a")
    va_t = _projt(xn, va.T, seq=seq, name="projt_v_a")
    qi_t = _projt(xn, qi.T, seq=seq, rope=tabs32_t, rope_half=half32, name="projt_q_idx")
    w_t = _projt(xn, wi.T, seq=seq, scale=IDX_DIM ** -0.5 * IDX_HEADS ** -0.5, out_dtype=jnp.float32, name="projt_w_idx")
    out_a = _dsa(qa_t, k_a, va_t, qi_t, k_i, w_t, batch=batch, seq=seq)
    out_b = _dilated(qk_b.reshape(batch, seq, -1), v_b.reshape(batch, seq, -1), batch=batch, seq=seq)
    x = _out_proj(out_a, out_b.reshape(batch * seq, -1), w_out, x)
    xn = _rmsnorm(x, norm_ffn, name="rmsnorm_ffn")
    return _ffn(xn, w_gate, w_up, w_down, x, layer)


def _odd_layer(x, batch, seq, layer, norm_mix, w_in, q_norm, w_uq, kv_norm, w_ukv, conv_w, conv_b, w_a, b_a, w_x, b_x, lam,
               w_out, norm_ffn, w_router, w_gate, w_up, w_down, tabs_mla):
    cq, ckv, kr, xr, gr = jnp.split(w_in, np.cumsum([Q_LORA, KV_LORA, C_ROPE, D_RNN]).tolist(), axis=1)
    w_plain = jnp.concatenate([xr, gr, cq, ckv], axis=1)
    xn = _rmsnorm(x, norm_mix, name="rmsnorm_mix")
    po = _proj(xn, w_plain, seq=seq, name="proj_plain")
    k_rope = _proj(xn, _pad_cols(kr, LANES), seq=seq, rope=tabs_mla, rope_half=C_ROPE // 2,
                   out_dtype=jnp.bfloat16, name="proj_krope")
    cqn = _rmsnorm(po, q_norm, width=Q_LORA, col_block=2 * D_RNN // Q_LORA, name="rmsnorm_cq")
    ckvn = _rmsnorm(po, kv_norm, width=KV_LORA, col_block=(2 * D_RNN + Q_LORA) // KV_LORA, name="rmsnorm_ckv")
    w_uq = w_uq.reshape(Q_LORA, C_HEADS, C_NOPE + C_ROPE)
    w_qn_t = w_uq[:, :, :C_NOPE].reshape(Q_LORA, C_HEADS * C_NOPE).T
    w_qr_t = w_uq[:, :, C_NOPE:].reshape(Q_LORA, C_HEADS * C_ROPE).T
    w_ukv = w_ukv.reshape(KV_LORA, C_HEADS, C_NOPE + C_V)
    w_kn = w_ukv[:, :, :C_NOPE].reshape(KV_LORA, C_HEADS * C_NOPE)
    w_v_t = w_ukv[:, :, C_NOPE:].reshape(KV_LORA, C_HEADS * C_V).T
    q_scale = (C_NOPE + C_ROPE) ** -0.5 * LOG2_E
    tabs_mla_t = tuple(t.T for t in tabs_mla)
    qn_t = _projt(cqn, w_qn_t, seq=seq, scale=q_scale, name="projt_qnope")
    qr_t = _projt(cqn, w_qr_t, seq=seq, rope=tabs_mla_t, rope_half=C_ROPE // 2, scale=q_scale, name="projt_qrope")
    kn = _proj(ckvn, w_kn, seq=seq, out_dtype=jnp.bfloat16, name="proj_knope")
    v_t = _projt(ckvn, w_v_t, seq=seq, name="projt_v")
    r3 = lambda a: a.reshape(batch, seq, -1)
    out_c = _mla(qn_t, qr_t, kn, k_rope, v_t, batch=batch, seq=seq)
    out_d = _rglru(r3(po), conv_w, conv_b, w_a, b_a, w_x, b_x, lam, batch=batch, seq=seq).reshape(batch * seq, -1)
    x = _out_proj(out_c, out_d, w_out, x)
    return _moe(x, norm_ffn, w_router, w_gate, w_up, w_down, layer)


def kernel(x, ev_norm_mix, ev_w_in, ev_w_out, ev_norm_ffn, ffn_w_gate, ffn_w_up, ffn_w_down, od_norm_mix, od_w_in, mla_q_norm, mla_w_uq, mla_kv_norm, mla_w_ukv, rg_conv_w, rg_conv_b, rg_w_a, rg_b_a, rg_w_x, rg_b_x, rg_lambda, od_w_out, od_norm_ffn, moe_router, moe_w_gate, moe_w_up, moe_w_down, final_norm):
    batch, seq, d = x.shape
    depth = ev_w_in.shape[0] + od_w_in.shape[0]
    assert d == D_MODEL and seq % KEY_CHUNK == 0 and (batch * seq) % FFN_ROW_TILE == 0
    tabs64 = _rope_tables(seq, HEAD_DIM, ROT_DIM)
    tabs32 = _rope_tables(seq, IDX_DIM, IDX_ROT)
    tabs_mla = _rope_tables(seq, C_ROPE, C_ROPE)
    h = x.reshape(batch * seq, d)
    for layer in range(depth):
        i = layer // 2
        if layer % 2 == 0:
            h = _even_layer(h, batch, seq, i, ev_norm_mix[i], ev_w_in[i], ev_w_out[i], ev_norm_ffn[i],
                            ffn_w_gate, ffn_w_up, ffn_w_down, tabs64, tabs32)
        else:
            h = _odd_layer(h, batch, seq, i, od_norm_mix[i], od_w_in[i], mla_q_norm[i], mla_w_uq[i], mla_kv_norm[i],
                           mla_w_ukv[i], rg_conv_w[i], rg_conv_b[i], rg_w_a[i], rg_b_a[i], rg_w_x[i], rg_b_x[i],
                           rg_lambda[i], od_w_out[i], od_norm_ffn[i], moe_router[i], moe_w_gate, moe_w_up,
                           moe_w_down, tabs_mla)
    out = _rmsnorm(h, final_norm, out_dtype=jnp.float32, name="rmsnorm_final")
    return out.reshape(batch, seq, d)
```

```python
import functools

import jax
import jax.numpy as jnp
import numpy as np
from jax import lax
from jax.experimental import pallas as pl
from jax.experimental.pallas import tpu as pltpu

D_MODEL = 1024
HEAD_DIM = 64
ROT_DIM = HEAD_DIM // 4
ROPE_THETA = 500000.0
NORM_EPS = 1e-6

A_HEADS = 8
A_KV_HEADS = 2
IDX_HEADS = 8
IDX_DIM = 32
IDX_ROT = IDX_DIM // 4
TOPK_MAX = 256

B_HEADS = 8
DILATED_PATTERNS = ((128, 1), (512, 4), (2048, 16))

C_HEADS = 8
C_NOPE = 64
C_ROPE = 32
C_V = 64
Q_LORA = 256
KV_LORA = 128

D_RNN = 512
RG_BLOCKS = 8
RG_BW = D_RNN // RG_BLOCKS
CONV_W = 4
RG_C = 8.0

D_FF = 3584
N_EXPERTS = 8
TOP_K = 2

LANES = 128
VMEM_LIMIT_BYTES = 56 * 1024 * 1024
BLOCK_Q = 128
KEY_CHUNK = 512
MLA_BLOCK_Q = 256
DSA_BLOCK_Q = 256
SUBLANES = 8
LOG2_E = 1.4426950408889634
ROW_TILE = 512
FFN_ROW_TILE = 1024
FFN_COL_TILE = 512
MOE_ROW_TILE = 1024
GATHER_ROWS = 256
DISPATCH_ROWS = 512
RG_CHUNK = 256
NEG_BIG = -1e30
INT_MIN = -2147483648

_NT = (((1,), (1,)), ((), ()))


def _params(semantics, vmem=VMEM_LIMIT_BYTES):
    return pltpu.CompilerParams(dimension_semantics=semantics, vmem_limit_bytes=vmem)


def _bf16(a):
    return a if a.dtype == jnp.bfloat16 else a.astype(jnp.bfloat16)


def _rmsnorm_kernel(x_ref, g_ref, o_ref):
    x = x_ref[...].astype(jnp.float32)
    ms = jnp.mean(x * x, axis=-1, keepdims=True)
    o_ref[...] = (x * lax.rsqrt(ms + NORM_EPS) * g_ref[...]).astype(o_ref.dtype)


def _rmsnorm(x, g, *, out_dtype, name):
    n, width = x.shape
    return pl.pallas_call(
        _rmsnorm_kernel,
        grid=(n // ROW_TILE,),
        in_specs=[pl.BlockSpec((ROW_TILE, width), lambda i: (i, 0)),
                  pl.BlockSpec((1, width), lambda i: (0, 0))],
        out_specs=pl.BlockSpec((ROW_TILE, width), lambda i: (i, 0)),
        out_shape=jax.ShapeDtypeStruct((n, width), out_dtype),
        compiler_params=_params(("parallel",)),
        name=name,
    )(x, g.reshape(1, width).astype(jnp.float32))


def _rope_tables(seq, period, rot_dim):
    half = rot_dim // 2
    inv_freq = 1.0 / (ROPE_THETA ** (jnp.arange(half, dtype=jnp.float32) * (2.0 / rot_dim)))
    ang = jnp.arange(seq, dtype=jnp.float32)[:, None] * inv_freq[None, :]
    cos, sin = jnp.cos(ang), jnp.sin(ang)
    rest = period - rot_dim
    ones = jnp.ones((seq, rest), jnp.float32)
    zeros = jnp.zeros((seq, rest), jnp.float32)
    zh = jnp.zeros((seq, half), jnp.float32)
    c = jnp.concatenate([cos, cos, ones], axis=1)
    sa = jnp.concatenate([-sin, zh, zeros], axis=1)
    sb = jnp.concatenate([zh, sin, zeros], axis=1)
    reps = LANES // period
    return tuple(jnp.tile(t, (1, reps)) for t in (c, sa, sb))


def _out_proj_kernel(a1_ref, a2_ref, w1_ref, w2_ref, res_ref, o_ref):
    y = jnp.dot(_bf16(a1_ref[...]), _bf16(w1_ref[...]), preferred_element_type=jnp.float32)
    y = y + jnp.dot(_bf16(a2_ref[...]), _bf16(w2_ref[...]), preferred_element_type=jnp.float32)
    o_ref[...] = res_ref[...] + y


def _out_proj(a1, a2, w, res):
    n, k1 = a1.shape
    k2 = a2.shape[1]
    d = w.shape[1]
    tm = ROW_TILE
    return pl.pallas_call(
        _out_proj_kernel,
        grid=(n // tm,),
        in_specs=[pl.BlockSpec((tm, k1), lambda i: (i, 0)),
                  pl.BlockSpec((tm, k2), lambda i: (i, 0)),
                  pl.BlockSpec((k1, d), lambda i: (0, 0)),
                  pl.BlockSpec((k2, d), lambda i: (0, 0)),
                  pl.BlockSpec((tm, d), lambda i: (i, 0))],
        out_specs=pl.BlockSpec((tm, d), lambda i: (i, 0)),
        out_shape=jax.ShapeDtypeStruct((n, d), jnp.float32),
        compiler_params=_params(("parallel",)),
        name="proj_out",
    )(a1, a2, w[:k1], w[k1:], res)


def _rms(x, g):
    ms = jnp.mean(x * x, axis=-1, keepdims=True)
    return x * lax.rsqrt(ms + NORM_EPS) * g


def _rope_cols(y, tabs, half):
    width = y.shape[1]
    c, sa, sb = (jnp.tile(t[...], (1, width // LANES)) for t in tabs)
    return y * c + pltpu.roll(y, width - half, 1) * sa + pltpu.roll(y, half, 1) * sb


def _rope_rows(y, tabs_t, half):
    rows = y.shape[0]
    c, sa, sb = (jnp.tile(t[...], (rows // LANES, 1)) for t in tabs_t)
    return y * c + pltpu.roll(y, rows - half, 0) * sa + pltpu.roll(y, half, 0) * sb


def _even_in_kernel(x_ref, g_ref, wqk_ref, wka_ref, wki_ref, wvb_ref, wqat_ref, wvat_ref, wqit_ref, wwt_ref,
                    c64, sa64, sb64, c32, sa32, sb32, c64t, sa64t, sb64t, c32t, sa32t, sb32t,
                    qk_o, ka_o, ki_o, vb_o, qat_o, vat_o, qit_o, wt_o):
    xn = _rms(x_ref[...], g_ref[...]).astype(jnp.bfloat16)
    nn = lambda w: jnp.dot(xn, w[...], preferred_element_type=jnp.float32)
    nt = lambda wt: lax.dot_general(wt[...], xn, _NT, preferred_element_type=jnp.float32)
    half64, half32 = ROT_DIM // 2, IDX_ROT // 2
    qk_o[...] = _rope_cols(nn(wqk_ref), (c64, sa64, sb64), half64)
    ka_o[...] = _rope_cols(nn(wka_ref), (c64, sa64, sb64), half64).astype(ka_o.dtype)
    ki_o[...] = _rope_cols(nn(wki_ref), (c32, sa32, sb32), half32).astype(ki_o.dtype)
    vb_o[...] = nn(wvb_ref)
    qat_o[0] = (_rope_rows(nt(wqat_ref), (c64t, sa64t, sb64t), half64) * (HEAD_DIM ** -0.5 * LOG2_E)).astype(qat_o.dtype)
    vat_o[0] = nt(wvat_ref).astype(vat_o.dtype)
    qit_o[0] = _rope_rows(nt(wqit_ref), (c32t, sa32t, sb32t), half32).astype(qit_o.dtype)
    wt_o[0] = nt(wwt_ref) * (IDX_DIM ** -0.5 * IDX_HEADS ** -0.5)


def _resident(shape):
    return pl.BlockSpec(shape, lambda i: (0,) * len(shape))


def _even_in(x, g, w_in, tabs64, tabs32, *, seq):
    n, d = x.shape
    tm = ROW_TILE
    tiles = n // tm
    seq_blocks = seq // tm
    bf = jnp.bfloat16
    qa, ka, va, qi, ki, wi, qb, kb, vb = jnp.split(
        w_in.astype(bf), np.cumsum([512, 128, 128, 256, 32, 8, 512, 512]).tolist(), axis=1)
    weights = [jnp.concatenate([qb, kb], axis=1), ka, _pad_cols(ki, LANES), vb, qa.T, va.T, qi.T, wi.T]
    tabs = list(tabs64) + list(tabs32)
    tabs_t = [t.T for t in tabs]
    tok = lambda w: pl.BlockSpec((tm, w), lambda i: (i, 0))
    feat = lambda r: pl.BlockSpec((1, r, tm), lambda i: (i, 0, 0))
    out_shapes = [((n, 2 * B_HEADS * HEAD_DIM), jnp.float32, tok), ((n, LANES), bf, tok), ((n, LANES), bf, tok),
                  ((n, B_HEADS * HEAD_DIM), jnp.float32, tok),
                  ((tiles, A_HEADS * HEAD_DIM, tm), bf, feat), ((tiles, A_KV_HEADS * HEAD_DIM, tm), bf, feat),
                  ((tiles, IDX_HEADS * IDX_DIM, tm), bf, feat), ((tiles, IDX_HEADS, tm), jnp.float32, feat)]
    return pl.pallas_call(
        _even_in_kernel,
        grid=(tiles,),
        in_specs=([tok(d), _resident((1, d))] + [_resident(w.shape) for w in weights]
                  + [pl.BlockSpec((tm, LANES), lambda i: (i % seq_blocks, 0))] * 6
                  + [pl.BlockSpec((LANES, tm), lambda i: (0, i % seq_blocks))] * 6),
        out_specs=[mk(shape[1]) for shape, _, mk in out_shapes],
        out_shape=[jax.ShapeDtypeStruct(shape, dt) for shape, dt, _ in out_shapes],
        compiler_params=_params(("parallel",)),
        name="even_in_proj",
    )(x, g.reshape(1, d).astype(jnp.float32), *weights, *tabs, *tabs_t)


def _odd_in_kernel(x_ref, g_ref, wplain_ref, wkr_ref, qg_ref, kvg_ref, wqnt_ref, wqrt_ref, wkn_ref, wvt_ref,
                   c, sa, sb, ct, sat, sbt, xg_o, kr_o, qnt_o, qrt_o, kn_o, vt_o):
    xn = _rms(x_ref[...], g_ref[...]).astype(jnp.bfloat16)
    po = jnp.dot(xn, wplain_ref[...], preferred_element_type=jnp.float32)
    rnn = 2 * D_RNN
    xg_o[...] = po[:, :rnn]
    half = C_ROPE // 2
    kr = jnp.dot(xn, wkr_ref[...], preferred_element_type=jnp.float32)
    kr_o[...] = _rope_cols(kr, (c, sa, sb), half).astype(kr_o.dtype)
    cqn = _rms(po[:, rnn:rnn + Q_LORA], qg_ref[...]).astype(jnp.bfloat16)
    ckvn = _rms(po[:, rnn + Q_LORA:], kvg_ref[...]).astype(jnp.bfloat16)
    q_scale = (C_NOPE + C_ROPE) ** -0.5 * LOG2_E
    nt = lambda wt, a: lax.dot_general(wt[...], a, _NT, preferred_element_type=jnp.float32)
    qnt_o[0] = (nt(wqnt_ref, cqn) * q_scale).astype(qnt_o.dtype)
    qrt_o[0] = (_rope_rows(nt(wqrt_ref, cqn), (ct, sat, sbt), half) * q_scale).astype(qrt_o.dtype)
    kn_o[...] = jnp.dot(ckvn, wkn_ref[...], preferred_element_type=jnp.float32).astype(kn_o.dtype)
    vt_o[0] = nt(wvt_ref, ckvn).astype(vt_o.dtype)


def _odd_in(x, g, w_in, q_norm, w_uq, kv_norm, w_ukv, tabs_mla, *, seq):
    n, d = x.shape
    tm = ROW_TILE
    tiles = n // tm
    seq_blocks = seq // tm
    bf = jnp.bfloat16
    cq, ckv, kr, xr, gr = jnp.split(w_in.astype(bf), np.cumsum([Q_LORA, KV_LORA, C_ROPE, D_RNN]).tolist(), axis=1)
    w_uq = w_uq.astype(bf).reshape(Q_LORA, C_HEADS, C_NOPE + C_ROPE)
    w_ukv = w_ukv.astype(bf).reshape(KV_LORA, C_HEADS, C_NOPE + C_V)
    weights = [jnp.concatenate([xr, gr, cq, ckv], axis=1), _pad_cols(kr, LANES),
               q_norm.reshape(1, Q_LORA).astype(jnp.float32), kv_norm.reshape(1, KV_LORA).astype(jnp.float32),
               w_uq[:, :, :C_NOPE].reshape(Q_LORA, C_HEADS * C_NOPE).T,
               w_uq[:, :, C_NOPE:].reshape(Q_LORA, C_HEADS * C_ROPE).T,
               w_ukv[:, :, :C_NOPE].reshape(KV_LORA, C_HEADS * C_NOPE),
               w_ukv[:, :, C_NOPE:].reshape(KV_LORA, C_HEADS * C_V).T]
    tabs = list(tabs_mla)
    tabs_t = [t.T for t in tabs]
    tok = lambda w: pl.BlockSpec((tm, w), lambda i: (i, 0))
    feat = lambda r: pl.BlockSpec((1, r, tm), lambda i: (i, 0, 0))
    out_shapes = [((n, 2 * D_RNN), jnp.float32, tok), ((n, LANES), bf, tok),
                  ((tiles, C_HEADS * C_NOPE, tm), bf, feat), ((tiles, C_HEADS * C_ROPE, tm), bf, feat),
                  ((n, C_HEADS * C_NOPE), bf, tok), ((tiles, C_HEADS * C_V, tm), bf, feat)]
    return pl.pallas_call(
        _odd_in_kernel,
        grid=(tiles,),
        in_specs=([tok(d), _resident((1, d))] + [_resident(w.shape) for w in weights]
                  + [pl.BlockSpec((tm, LANES), lambda i: (i % seq_blocks, 0))] * 3
                  + [pl.BlockSpec((LANES, tm), lambda i: (0, i % seq_blocks))] * 3),
        out_specs=[mk(shape[1]) for shape, _, mk in out_shapes],
        out_shape=[jax.ShapeDtypeStruct(shape, dt) for shape, dt, _ in out_shapes],
        compiler_params=_params(("parallel",)),
        name="odd_in_proj",
    )(x, g.reshape(1, d).astype(jnp.float32), *weights, *tabs, *tabs_t)


def _silu(g):
    return g * (1.0 / (1.0 + jnp.exp(-g)))


def _ffn_kernel(x_ref, g_ref, wg_ref, wu_ref, wd_ref, o_ref, acc_ref, xn_ref):
    f = pl.program_id(1)

    @pl.when(f == 0)
    def _():
        acc_ref[...] = jnp.zeros_like(acc_ref)
        xn_ref[...] = _rms(x_ref[...], g_ref[...]).astype(xn_ref.dtype)

    xn = xn_ref[...]
    g = jnp.dot(xn, _bf16(wg_ref[0]), preferred_element_type=jnp.float32)
    u = jnp.dot(xn, _bf16(wu_ref[0]), preferred_element_type=jnp.float32)
    h = (_silu(g) * u).astype(jnp.bfloat16)
    acc_ref[...] += jnp.dot(h, _bf16(wd_ref[0]), preferred_element_type=jnp.float32)

    @pl.when(f == pl.num_programs(1) - 1)
    def _():
        o_ref[...] = x_ref[...] + acc_ref[...]


def _ffn(x, g, w_gate, w_up, w_down, layer):
    n, d = x.shape
    ff = w_gate.shape[2]
    tm, tf = FFN_ROW_TILE, FFN_COL_TILE
    return pl.pallas_call(
        _ffn_kernel,
        grid=(n // tm, ff // tf),
        in_specs=[pl.BlockSpec((tm, d), lambda i, f: (i, 0)),
                  pl.BlockSpec((1, d), lambda i, f: (0, 0)),
                  pl.BlockSpec((1, d, tf), lambda i, f: (layer, 0, f)),
                  pl.BlockSpec((1, d, tf), lambda i, f: (layer, 0, f)),
                  pl.BlockSpec((1, tf, d), lambda i, f: (layer, f, 0))],
        out_specs=pl.BlockSpec((tm, d), lambda i, f: (i, 0)),
        out_shape=jax.ShapeDtypeStruct((n, d), jnp.float32),
        scratch_shapes=[pltpu.VMEM((tm, d), jnp.float32), pltpu.VMEM((tm, d), jnp.bfloat16)],
        compiler_params=_params(("parallel", "arbitrary")),
        name="ffn_swiglu",
    )(x, g.reshape(1, d).astype(jnp.float32), w_gate, w_up, w_down)


def _dsa_kernel(qt_ref, k_ref, vt_ref, qit_ref, ki_ref, wt_ref, o_ref, key_scr, bias_scr, *scr, topk):
    acc_scr, m_scr, l_scr = scr[:A_HEADS], scr[A_HEADS:2 * A_HEADS], scr[2 * A_HEADS:]
    i = pl.program_id(1)
    tq = DSA_BLOCK_Q
    n_chunks = (i * tq + tq + KEY_CHUNK - 1) // KEY_CHUNK
    qpos = i * tq + lax.broadcasted_iota(jnp.int32, (1, tq), 1)
    kloc = lax.broadcasted_iota(jnp.int32, (KEY_CHUNK, 1), 0)

    w = wt_ref[0]

    def index_body(c, carry):
        kc = ki_ref[pl.ds(c * KEY_CHUNK, KEY_CHUNK), :][:, :IDX_DIM]
        score = jnp.zeros((KEY_CHUNK, tq), jnp.float32)
        for h in range(IDX_HEADS):
            lg = jnp.dot(kc, qit_ref[0, h * IDX_DIM:(h + 1) * IDX_DIM, :], preferred_element_type=jnp.float32)
            score = score + w[h:h + 1, :] * jnp.maximum(lg, 0.0)
        bits = pltpu.bitcast(score, jnp.int32)
        key = jnp.where(bits < 0, bits ^ jnp.int32(0x7FFFFFFF), bits)
        key_scr[c] = jnp.where(c * KEY_CHUNK + kloc <= qpos, key, jnp.int32(INT_MIN))
        return carry

    lax.fori_loop(0, n_chunks, index_body, 0)

    def count(pred):
        def body(c, acc):
            hit = pred(key_scr[c]).astype(jnp.int32)
            return acc + jnp.sum(hit.reshape(KEY_CHUNK // SUBLANES, SUBLANES, tq), axis=0)
        acc = lax.fori_loop(0, n_chunks, body, jnp.zeros((SUBLANES, tq), jnp.int32))
        return jnp.sum(acc, axis=0, keepdims=True)

    def bit_body(t, res):
        cand = res | jnp.left_shift(jnp.int32(1), 31 - t)
        cand_key = cand ^ jnp.int32(INT_MIN)
        return jnp.where(count(lambda key: key >= cand_key) >= topk, cand, res)

    thr = lax.fori_loop(0, 32, bit_body, jnp.zeros((1, tq), jnp.int32)) ^ jnp.int32(INT_MIN)
    need = (topk - count(lambda key: key > thr)).astype(jnp.float32)

    tri = (lax.broadcasted_iota(jnp.int32, (KEY_CHUNK, KEY_CHUNK), 1)
           <= lax.broadcasted_iota(jnp.int32, (KEY_CHUNK, KEY_CHUNK), 0)).astype(jnp.bfloat16)

    def mask_body(c, seen):
        key = key_scr[c]
        eq = key == thr
        running = jnp.dot(tri, eq.astype(jnp.bfloat16), preferred_element_type=jnp.float32) + seen
        sel = ((key > thr) | (eq & (running <= need))) & (c * KEY_CHUNK + kloc <= qpos)
        bias_scr[c] = jnp.where(sel, 0.0, NEG_BIG)
        return running[KEY_CHUNK - 1:KEY_CHUNK, :]

    lax.fori_loop(0, n_chunks, mask_body, jnp.zeros((1, tq), jnp.float32))

    for h in range(A_HEADS):
        acc_scr[h][...] = jnp.zeros_like(acc_scr[h])
        l_scr[h][...] = jnp.zeros_like(l_scr[h])
        m_scr[h][...] = jnp.full_like(m_scr[h], NEG_BIG)
    group = A_HEADS // A_KV_HEADS

    def att_body(c, carry):
        rows = pl.ds(c * KEY_CHUNK, KEY_CHUNK)
        bias = bias_scr[c]
        scores = []
        for h in range(A_HEADS):
            n = h // group
            scores.append(jnp.dot(k_ref[rows, n * HEAD_DIM:(n + 1) * HEAD_DIM],
                                  qt_ref[0, h * HEAD_DIM:(h + 1) * HEAD_DIM, :],
                                  preferred_element_type=jnp.float32))
        probs, alphas = [], []
        for h in range(A_HEADS):
            s = scores[h] + bias
            m_old = m_scr[h][...]
            m_new = jnp.maximum(m_old, jnp.max(s, axis=0, keepdims=True))
            alpha = jnp.exp2(m_old - m_new)
            p = jnp.exp2(s - m_new)
            l_scr[h][...] = l_scr[h][...] * alpha + jnp.sum(p, axis=0, keepdims=True)
            m_scr[h][...] = m_new
            probs.append(p.astype(jnp.bfloat16))
            alphas.append(alpha)
        for h in range(A_HEADS):
            n = h // group
            acc_scr[h][...] = acc_scr[h][...] * alphas[h] + jnp.dot(
                vt_ref[c, n * HEAD_DIM:(n + 1) * HEAD_DIM, :], probs[h], preferred_element_type=jnp.float32)
        return carry

    lax.fori_loop(0, n_chunks, att_body, 0)
    out_t = jnp.concatenate([acc_scr[h][...] * (1.0 / l_scr[h][...]) for h in range(A_HEADS)], axis=0)
    o_ref[...] = out_t.T


def _dsa(qa_t, ka, va_t, qi_t, ki, w_t, *, batch, seq):
    topk = min(TOPK_MAX, seq // 4)
    tq = DSA_BLOCK_Q
    tiles = seq // ROW_TILE
    per_tile = ROW_TILE // tq
    n_chunks = seq // KEY_CHUNK
    qw = A_HEADS * HEAD_DIM
    kvw = A_KV_HEADS * HEAD_DIM
    q_tile = lambda b, i: (b * tiles + i // per_tile, 0, i % per_tile)
    return pl.pallas_call(
        functools.partial(_dsa_kernel, topk=topk),
        grid=(batch, seq // tq),
        in_specs=[pl.BlockSpec((1, qw, tq), q_tile),
                  pl.BlockSpec((seq, kvw), lambda b, i: (b, 0)),
                  pl.BlockSpec((tiles, kvw, ROW_TILE), lambda b, i: (b, 0, 0)),
                  pl.BlockSpec((1, IDX_HEADS * IDX_DIM, tq), q_tile),
                  pl.BlockSpec((seq, LANES), lambda b, i: (b, 0)),
                  pl.BlockSpec((1, IDX_HEADS, tq), q_tile)],
        out_specs=pl.BlockSpec((tq, qw), lambda b, i: (b * (seq // tq) + i, 0)),
        out_shape=jax.ShapeDtypeStruct((batch * seq, qw), jnp.float32),
        scratch_shapes=([pltpu.VMEM((n_chunks, KEY_CHUNK, tq), jnp.int32),
                         pltpu.VMEM((n_chunks, KEY_CHUNK, tq), jnp.float32)]
                        + [pltpu.VMEM((HEAD_DIM, tq), jnp.float32)] * A_HEADS
                        + [pltpu.VMEM((1, tq), jnp.float32)] * (2 * A_HEADS)),
        compiler_params=_params(("parallel", "arbitrary")),
        name="dsa_attention",
    )(qa_t, ka, va_t, qi_t, ki, w_t)


def _dilated_kernel(*refs, seq):
    pairs = B_HEADS // 2
    q_refs, k_refs, v_refs = refs[:pairs], refs[pairs:2 * pairs], refs[2 * pairs:3 * pairs]
    o_ref = refs[3 * pairs]
    scr = refs[3 * pairs + 1:]
    acc_scr, m_scr, l_scr = scr[:pairs], scr[pairs:2 * pairs], scr[2 * pairs:]
    scale = HEAD_DIM ** -0.5
    row = lax.broadcasted_iota(jnp.int32, (BLOCK_Q, BLOCK_Q), 0)
    col = lax.broadcasted_iota(jnp.int32, (BLOCK_Q, BLOCK_Q), 1)
    mask_cur = col <= row
    mask_prev = col >= row

    for p in range(pairs):
        acc_scr[p][...] = jnp.zeros_like(acc_scr[p])
        l_scr[p][...] = jnp.zeros_like(l_scr[p])
        m_scr[p][...] = jnp.full_like(m_scr[p], NEG_BIG)

    def gather(ref_list, rows):
        return jnp.concatenate([_bf16(r[0, rows, :]) for r in ref_list], axis=1)

    def spread(cols):
        return jnp.concatenate([jnp.broadcast_to(c, (BLOCK_Q, HEAD_DIM)) for c in cols], axis=1)

    for (window, dil) in DILATED_PATTERNS:
        assert window // dil == BLOCK_Q and seq % (dil * BLOCK_Q) == 0
        n_blocks = seq // (dil * BLOCK_Q)

        def block_body(idx, carry, dil=dil, n_blocks=n_blocks):
            phase = idx // n_blocks
            blk = idx % n_blocks
            def view_rows(block):
                if dil == 1:
                    return pl.ds(pl.multiple_of(BLOCK_Q * block, BLOCK_Q), BLOCK_Q)
                return pl.ds(phase + dil * BLOCK_Q * block, BLOCK_Q, stride=dil)

            rows = view_rows(blk)
            q2 = gather(q_refs, rows)
            k_cur = gather(k_refs, rows)
            v_cur = gather(v_refs, rows)
            if n_blocks > 1:
                rows_p = view_rows(jnp.maximum(blk - 1, 0))
                k_prev = gather(k_refs, rows_p)
                v_prev = gather(v_refs, rows_p)
                mask_p = mask_prev & (blk > 0)
            m_old = jnp.concatenate([m[rows, :] for m in m_scr], axis=1)
            heads = q2.shape[1] // HEAD_DIM
            lanes = [slice(h * HEAD_DIM, (h + 1) * HEAD_DIM) for h in range(heads)]
            s_cur = [lax.dot_general(q2[:, sl], k_cur[:, sl], _NT, preferred_element_type=jnp.float32) for sl in lanes]
            if n_blocks > 1:
                s_prev = [lax.dot_general(q2[:, sl], k_prev[:, sl], _NT, preferred_element_type=jnp.float32)
                          for sl in lanes]
            new_m, new_sum, p_cur, p_prev = [], [], [], []
            for h in range(heads):
                s_c = jnp.where(mask_cur, s_cur[h] * scale, NEG_BIG)
                mx = jnp.max(s_c, axis=1, keepdims=True)
                if n_blocks > 1:
                    s_p = jnp.where(mask_p, s_prev[h] * scale, NEG_BIG)
                    mx = jnp.maximum(mx, jnp.max(s_p, axis=1, keepdims=True))
                m_h = jnp.maximum(m_old[:, h * HEAD_DIM:h * HEAD_DIM + 1], mx)
                p_c = jnp.exp(s_c - m_h)
                psum = jnp.sum(p_c, axis=1, keepdims=True)
                p_cur.append(p_c.astype(jnp.bfloat16))
                if n_blocks > 1:
                    p_p = jnp.exp(s_p - m_h)
                    psum = psum + jnp.sum(p_p, axis=1, keepdims=True)
                    p_prev.append(p_p.astype(jnp.bfloat16))
                new_m.append(m_h)
                new_sum.append(psum)
            new_pv = []
            for h in range(heads):
                pv = jnp.dot(p_cur[h], v_cur[:, lanes[h]], preferred_element_type=jnp.float32)
                if n_blocks > 1:
                    pv = pv + jnp.dot(p_prev[h], v_prev[:, lanes[h]], preferred_element_type=jnp.float32)
                new_pv.append(pv)
            m_new = spread(new_m)
            alpha = jnp.exp(m_old - m_new)
            sums = spread(new_sum)
            pvs = jnp.concatenate(new_pv, axis=1)
            for p in range(pairs):
                cols = slice(p * LANES, (p + 1) * LANES)
                m_scr[p][rows, :] = m_new[:, cols]
                l_scr[p][rows, :] = l_scr[p][rows, :] * alpha[:, cols] + sums[:, cols]
                acc_scr[p][rows, :] = acc_scr[p][rows, :] * alpha[:, cols] + pvs[:, cols]
            return carry

        lax.fori_loop(0, dil * n_blocks, block_body, 0)

    for p in range(pairs):
        o_ref[0, :, p * LANES:(p + 1) * LANES] = acc_scr[p][...] * (1.0 / l_scr[p][...])


def _dilated(qk, v, *, batch, seq):
    width = B_HEADS * HEAD_DIM
    pairs = B_HEADS // 2

    def col_block(j):
        return pl.BlockSpec((1, seq, LANES), lambda b: (b, 0, j))

    return pl.pallas_call(
        functools.partial(_dilated_kernel, seq=seq),
        grid=(batch,),
        in_specs=([col_block(p) for p in range(pairs)] + [col_block(pairs + p) for p in range(pairs)]
                  + [col_block(p) for p in range(pairs)]),
        out_specs=pl.BlockSpec((1, seq, width), lambda b: (b, 0, 0)),
        out_shape=jax.ShapeDtypeStruct((batch, seq, width), jnp.float32),
        scratch_shapes=[pltpu.VMEM((seq, LANES), jnp.float32)] * (3 * pairs),
        compiler_params=_params(("parallel",)),
        name="dilated_attention",
    )(*([qk] * (2 * pairs) + [v] * pairs))


def _mla_kernel(qn_ref, qr_ref, kn_ref, kr_ref, vt_ref, o_ref, *scr):
    acc_scr, m_scr, l_scr = scr[:C_HEADS], scr[C_HEADS:2 * C_HEADS], scr[2 * C_HEADS:]
    i = pl.program_id(1)
    tq = MLA_BLOCK_Q
    n_full = (i * tq + 1) // KEY_CHUNK
    n_all = (i * tq + tq + KEY_CHUNK - 1) // KEY_CHUNK
    qpos = i * tq + lax.broadcasted_iota(jnp.int32, (1, tq), 1)
    kloc = lax.broadcasted_iota(jnp.int32, (KEY_CHUNK, 1), 0)

    for h in range(C_HEADS):
        acc_scr[h][...] = jnp.zeros_like(acc_scr[h])
        l_scr[h][...] = jnp.zeros_like(l_scr[h])
        m_scr[h][...] = jnp.full_like(m_scr[h], NEG_BIG)

    def chunk(c, carry, masked):
        rows = pl.ds(c * KEY_CHUNK, KEY_CHUNK)
        kr = kr_ref[rows, :][:, :C_ROPE]
        visible = (c * KEY_CHUNK + kloc) <= qpos
        scores = []
        for h in range(C_HEADS):
            hn = slice(h * C_NOPE, (h + 1) * C_NOPE)
            scores.append(jnp.dot(kn_ref[rows, hn], qn_ref[0, hn, :], preferred_element_type=jnp.float32)
                          + jnp.dot(kr, qr_ref[0, h * C_ROPE:(h + 1) * C_ROPE, :],
                                    preferred_element_type=jnp.float32))
        probs, alphas = [], []
        for h in range(C_HEADS):
            s = jnp.where(visible, scores[h], NEG_BIG) if masked else scores[h]
            m_old = m_scr[h][...]
            m_new = jnp.maximum(m_old, jnp.max(s, axis=0, keepdims=True))
            alpha = jnp.exp2(m_old - m_new)
            p = jnp.exp2(s - m_new)
            l_scr[h][...] = l_scr[h][...] * alpha + jnp.sum(p, axis=0, keepdims=True)
            m_scr[h][...] = m_new
            probs.append(p.astype(jnp.bfloat16))
            alphas.append(alpha)
        for h in range(C_HEADS):
            hv = slice(h * C_V, (h + 1) * C_V)
            acc_scr[h][...] = acc_scr[h][...] * alphas[h] + jnp.dot(vt_ref[c, hv, :], probs[h],
                                                                     preferred_element_type=jnp.float32)
        return carry

    lax.fori_loop(0, n_full, functools.partial(chunk, masked=False), 0)
    lax.fori_loop(n_full, n_all, functools.partial(chunk, masked=True), 0)

    out_t = jnp.concatenate([acc_scr[h][...] * (1.0 / l_scr[h][...]) for h in range(C_HEADS)], axis=0)
    o_ref[...] = out_t.T


def _mla(qn_t, qr_t, kn, kr, v_t, *, batch, seq):
    tq = MLA_BLOCK_Q
    tiles = seq // ROW_TILE
    per_tile = ROW_TILE // tq
    kw = C_HEADS * C_NOPE
    vw = C_HEADS * C_V
    n = batch * seq
    return pl.pallas_call(
        _mla_kernel,
        grid=(batch, seq // tq),
        in_specs=[pl.BlockSpec((1, kw, tq), lambda b, i: (b * tiles + i // per_tile, 0, i % per_tile)),
                  pl.BlockSpec((1, C_HEADS * C_ROPE, tq), lambda b, i: (b * tiles + i // per_tile, 0, i % per_tile)),
                  pl.BlockSpec((seq, kw), lambda b, i: (b, 0)),
                  pl.BlockSpec((seq, LANES), lambda b, i: (b, 0)),
                  pl.BlockSpec((tiles, vw, ROW_TILE), lambda b, i: (b, 0, 0))],
        out_specs=pl.BlockSpec((tq, vw), lambda b, i: (b * (seq // tq) + i, 0)),
        out_shape=jax.ShapeDtypeStruct((n, vw), jnp.float32),
        scratch_shapes=([pltpu.VMEM((C_V, tq), jnp.float32)] * C_HEADS
                        + [pltpu.VMEM((1, tq), jnp.float32)] * (2 * C_HEADS)),
        compiler_params=_params(("parallel", "arbitrary")),
        name="mla_attention",
    )(qn_t, qr_t, kn, kr, v_t)


def _expm1(y):
    u = jnp.exp(y)
    safe = jnp.where(u == 1.0, 2.0, u)
    return jnp.where(u == 1.0, y, (u - 1.0) * y / jnp.log(safe))


def _gelu_tanh(x):
    return 0.5 * x * (1.0 + jnp.tanh(np.sqrt(2.0 / np.pi).astype(np.float32) * (x + 0.044715 * (x * x * x))))


def _rglru_kernel(x_ref, g_ref, cw_ref, cb_ref, wa_ref, ba_ref, wx_ref, bx_ref, lam_ref, o_ref, a_scr, u_scr, *, seq):
    cw = cw_ref[...]
    neg_c_softplus = -RG_C * jnp.log1p(jnp.exp(-jnp.abs(-lam_ref[...]))) - RG_C * jnp.maximum(-lam_ref[...], 0.0)
    hi = lax.Precision.HIGHEST
    sub = 8

    def chunk_body(ci, h):
        r0 = pl.multiple_of(ci * RG_CHUNK, RG_CHUNK)
        xa = x_ref[0, pl.ds(r0, RG_CHUNK), :]
        prev = x_ref[0, pl.ds(pl.multiple_of(jnp.maximum(r0 - sub, 0), sub), sub), :]
        prev = jnp.where(ci > 0, prev, 0.0)
        xcat = jnp.concatenate([prev, xa], axis=0)
        xc = cw[CONV_W - 1:CONV_W, :] * xa + cb_ref[...]
        for j in range(1, CONV_W):
            xc = xc + cw[CONV_W - 1 - j:CONV_W - j, :] * pltpu.roll(xcat, j, 0)[sub:, :]
        r = 1.0 / (1.0 + jnp.exp(-(jnp.dot(xc, wa_ref[...], precision=hi, preferred_element_type=jnp.float32)
                                   + ba_ref[...])))
        ig = 1.0 / (1.0 + jnp.exp(-(jnp.dot(xc, wx_ref[...], precision=hi, preferred_element_type=jnp.float32)
                                    + bx_ref[...])))
        log_a = neg_c_softplus * r
        a_scr[...] = jnp.exp(log_a)
        u_scr[...] = jnp.sqrt(-_expm1(2.0 * log_a)) * (ig * xc)
        gate = _gelu_tanh(g_ref[0, pl.ds(r0, RG_CHUNK), :])

        def group_body(gi, h):
            g0 = pl.multiple_of(gi * sub, sub)
            a8 = a_scr[pl.ds(g0, sub), :]
            u8 = u_scr[pl.ds(g0, sub), :]
            rows = []
            for t in range(sub):
                h = a8[t:t + 1, :] * h + u8[t:t + 1, :]
                rows.append(h)
            u_scr[pl.ds(g0, sub), :] = jnp.concatenate(rows, axis=0)
            return h

        h = lax.fori_loop(0, RG_CHUNK // sub, group_body, h)
        o_ref[0, pl.ds(r0, RG_CHUNK), :] = u_scr[...] * gate
        return h

    lax.fori_loop(0, seq // RG_CHUNK, chunk_body, jnp.zeros((1, D_RNN), jnp.float32))


def _block_diag(w):
    nb, bw, _ = w.shape
    eye = jnp.eye(nb, dtype=w.dtype)
    return (eye[:, None, :, None] * w[:, :, None, :]).reshape(nb * bw, nb * bw)


def _rglru(po, conv_w, conv_b, w_a, b_a, w_x, b_x, lam, *, batch, seq):
    row = lambda a: a.reshape(1, D_RNN).astype(jnp.float32)
    vec = pl.BlockSpec((1, D_RNN), lambda b: (0, 0))
    mat = pl.BlockSpec((D_RNN, D_RNN), lambda b: (0, 0))
    return pl.pallas_call(
        functools.partial(_rglru_kernel, seq=seq),
        grid=(batch,),
        in_specs=[pl.BlockSpec((1, seq, D_RNN), lambda b: (b, 0, 0)),
                  pl.BlockSpec((1, seq, D_RNN), lambda b: (b, 0, 1)),
                  pl.BlockSpec((CONV_W, D_RNN), lambda b: (0, 0)),
                  vec, mat, vec, mat, vec, vec],
        out_specs=pl.BlockSpec((1, seq, D_RNN), lambda b: (b, 0, 0)),
        out_shape=jax.ShapeDtypeStruct((batch, seq, D_RNN), jnp.float32),
        scratch_shapes=[pltpu.VMEM((RG_CHUNK, D_RNN), jnp.float32)] * 2,
        compiler_params=_params(("parallel",)),
        name="rglru",
    )(po, po, conv_w.astype(jnp.float32), row(conv_b), _block_diag(w_a), row(b_a), _block_diag(w_x), row(b_x), row(lam))


def _pack_bf16_pairs(x):
    w = x.shape[1] // 2
    hi = pltpu.bitcast(x[:, :w].astype(jnp.bfloat16).astype(jnp.float32), jnp.uint32)
    lo = pltpu.bitcast(x[:, w:].astype(jnp.bfloat16).astype(jnp.float32), jnp.uint32)
    return hi | (lo >> 16)


def _unpack_bf16_pairs(p):
    hi = pltpu.bitcast(p & jnp.uint32(0xFFFF0000), jnp.float32)
    lo = pltpu.bitcast(p << 16, jnp.float32)
    return jnp.concatenate([hi, lo], axis=1).astype(jnp.bfloat16)


def _router_kernel(x_ref, g_ref, wr_ref, xn_ref, idx_ref, gate_ref):
    x = x_ref[...]
    ms = jnp.mean(x * x, axis=-1, keepdims=True)
    xn = x * lax.rsqrt(ms + NORM_EPS) * g_ref[...]
    xn_ref[...] = _pack_bf16_pairs(xn)
    logits = jnp.dot(xn, wr_ref[...], precision=lax.Precision.HIGHEST, preferred_element_type=jnp.float32)
    lane = lax.broadcasted_iota(jnp.int32, logits.shape, 1)
    logits = jnp.where(lane < N_EXPERTS, logits, -jnp.inf)
    m1 = jnp.max(logits, axis=1, keepdims=True)
    i1 = jnp.min(jnp.where(logits == m1, lane, LANES), axis=1, keepdims=True)
    rest = jnp.where(lane == i1, -jnp.inf, logits)
    m2 = jnp.max(rest, axis=1, keepdims=True)
    i2 = jnp.min(jnp.where(rest == m2, lane, LANES), axis=1, keepdims=True)
    e = jnp.exp(m2 - m1)
    g1 = 1.0 / (1.0 + e)
    g2 = e / (1.0 + e)
    idx_ref[...] = jnp.where(lane == 0, i1, jnp.where(lane == 1, i2, 0))
    gate_ref[...] = jnp.where(lane == 0, g1, jnp.where(lane == 1, g2, 0.0))


def _router(x, g, w_router):
    n, d = x.shape
    wr = jnp.pad(w_router.astype(jnp.float32), ((0, 0), (0, LANES - N_EXPERTS)))
    tm = ROW_TILE
    return pl.pallas_call(
        _router_kernel,
        grid=(n // tm,),
        in_specs=[pl.BlockSpec((tm, d), lambda i: (i, 0)),
                  pl.BlockSpec((1, d), lambda i: (0, 0)),
                  pl.BlockSpec((d, LANES), lambda i: (0, 0))],
        out_specs=[pl.BlockSpec((tm, d // 2), lambda i: (i, 0)),
                   pl.BlockSpec((tm, LANES), lambda i: (i, 0)),
                   pl.BlockSpec((tm, LANES), lambda i: (i, 0))],
        out_shape=[jax.ShapeDtypeStruct((n, d // 2), jnp.uint32),
                   jax.ShapeDtypeStruct((n, LANES), jnp.int32),
                   jax.ShapeDtypeStruct((n, LANES), jnp.float32)],
        compiler_params=_params(("parallel",)),
        name="moe_router",
    )(x, g.reshape(1, d).astype(jnp.float32), wr)


def _row_copy(src_hbm, dst, src_row, dst_row, sem):
    return pltpu.make_async_copy(src_hbm.at[pl.ds(src_row, 1)], dst.at[pl.ds(dst_row, 1)], sem)


def _gather_kernel(tok_ref, x_hbm, o_ref, x_vmem, sem):
    @pl.when(pl.program_id(0) == 0)
    def _():
        copy = pltpu.make_async_copy(x_hbm, x_vmem, sem)
        copy.start()
        copy.wait()

    base = pl.program_id(0) * DISPATCH_ROWS

    def group(g, carry):
        r0 = pl.multiple_of(g * SUBLANES, SUBLANES)
        rows = [x_vmem[tok_ref[base + r0 + j]] for j in range(SUBLANES)]
        o_ref[pl.ds(r0, SUBLANES), :] = jnp.concatenate(rows, axis=0)
        return carry

    lax.fori_loop(0, DISPATCH_ROWS // SUBLANES, group, 0)


def _gather_rows(row_tok, x):
    n_rows = row_tok.shape[0]
    n, w = x.shape
    return pl.pallas_call(
        _gather_kernel,
        grid_spec=pltpu.PrefetchScalarGridSpec(
            num_scalar_prefetch=1,
            grid=(n_rows // DISPATCH_ROWS,),
            in_specs=[pl.BlockSpec(memory_space=pl.ANY)],
            out_specs=pl.BlockSpec((DISPATCH_ROWS, w), lambda i, tok: (i, 0)),
            scratch_shapes=[pltpu.VMEM((n, 1, w), x.dtype), pltpu.SemaphoreType.DMA(())]),
        out_shape=jax.ShapeDtypeStruct((n_rows, w), x.dtype),
        compiler_params=_params(("arbitrary",)),
        name="moe_gather",
    )(row_tok, x.reshape(n, 1, w))


def _experts_kernel(be_ref, nv_ref, x_ref, wg_ref, wu_ref, wd_ref, o_ref, acc_ref, xb_ref):
    i = pl.program_id(0)
    f = pl.program_id(1)
    last = pl.num_programs(1) - 1

    @pl.when(i < nv_ref[0])
    def _():
        @pl.when(f == 0)
        def _():
            acc_ref[...] = jnp.zeros_like(acc_ref)
            xb_ref[...] = _unpack_bf16_pairs(x_ref[...])

        xb = xb_ref[...]
        g = jnp.dot(xb, _bf16(wg_ref[0]), preferred_element_type=jnp.float32)
        u = jnp.dot(xb, _bf16(wu_ref[0]), preferred_element_type=jnp.float32)
        h = (_silu(g) * u).astype(jnp.bfloat16)
        acc_ref[...] += jnp.dot(h, _bf16(wd_ref[0]), preferred_element_type=jnp.float32)

        @pl.when(f == last)
        def _():
            o_ref[...] = acc_ref[...]

    @pl.when((i >= nv_ref[0]) & (f == last))
    def _():
        o_ref[...] = jnp.zeros_like(o_ref)


def _experts(blk_expert, n_valid, x_rows, w_gate, w_up, w_down):
    n_rows, dw = x_rows.shape
    d = 2 * dw
    ff = w_gate.shape[2]
    tm, tf = MOE_ROW_TILE, FFN_COL_TILE
    n_f = ff // tf

    def fcol(i, f, nv):
        return jnp.where(i < nv[0], f, n_f - 1)

    return pl.pallas_call(
        _experts_kernel,
        grid_spec=pltpu.PrefetchScalarGridSpec(
            num_scalar_prefetch=2,
            grid=(n_rows // tm, n_f),
            in_specs=[pl.BlockSpec((tm, dw), lambda i, f, be, nv: (i, 0)),
                      pl.BlockSpec((1, d, tf), lambda i, f, be, nv: (be[i], 0, fcol(i, f, nv))),
                      pl.BlockSpec((1, d, tf), lambda i, f, be, nv: (be[i], 0, fcol(i, f, nv))),
                      pl.BlockSpec((1, tf, d), lambda i, f, be, nv: (be[i], fcol(i, f, nv), 0))],
            out_specs=pl.BlockSpec((tm, d), lambda i, f, be, nv: (i, 0)),
            scratch_shapes=[pltpu.VMEM((tm, d), jnp.float32), pltpu.VMEM((tm, d), jnp.bfloat16)]),
        out_shape=jax.ShapeDtypeStruct((n_rows, d), jnp.float32),
        compiler_params=_params(("arbitrary", "arbitrary")),
        name="moe_experts",
    )(blk_expert, n_valid, x_rows, w_gate, w_up, w_down)


def _combine_kernel(dest_ref, y_hbm, x_ref, gate_ref, o_ref, buf, sem):
    base = pl.program_id(0) * GATHER_ROWS * TOP_K

    def issue(r, carry):
        for k in range(TOP_K):
            _row_copy(y_hbm, buf.at[k], dest_ref[base + r * TOP_K + k], r, sem).start()
        return carry

    lax.fori_loop(0, GATHER_ROWS, issue, 0)
    for k in range(TOP_K):
        pltpu.make_async_copy(y_hbm.at[pl.ds(0, GATHER_ROWS)], buf.at[k], sem).wait()
    gate = gate_ref[...]
    o_ref[...] = x_ref[...] + gate[:, 0:1] * buf[0] + gate[:, 1:2] * buf[1]


def _combine(dest, y_rows, x, gates):
    n, d = x.shape
    return pl.pallas_call(
        _combine_kernel,
        grid_spec=pltpu.PrefetchScalarGridSpec(
            num_scalar_prefetch=1,
            grid=(n // GATHER_ROWS,),
            in_specs=[pl.BlockSpec(memory_space=pl.ANY),
                      pl.BlockSpec((GATHER_ROWS, d), lambda i, dest: (i, 0)),
                      pl.BlockSpec((GATHER_ROWS, LANES), lambda i, dest: (i, 0))],
            out_specs=pl.BlockSpec((GATHER_ROWS, d), lambda i, dest: (i, 0)),
            scratch_shapes=[pltpu.VMEM((TOP_K, GATHER_ROWS, d), jnp.float32),
                            pltpu.SemaphoreType.DMA(())]),
        out_shape=jax.ShapeDtypeStruct((n, d), jnp.float32),
        compiler_params=_params(("arbitrary",)),
        name="moe_combine",
    )(dest, y_rows, x, gates)


def _moe(x, g, w_router, w_gate, w_up, w_down, layer):
    n = x.shape[0]
    w_gate, w_up, w_down = (w.reshape((-1,) + w.shape[2:]) for w in (w_gate, w_up, w_down))
    xn, idx, gates = _router(x, g, w_router)
    e_flat = idx[:, :TOP_K].reshape(-1)
    onehot = (e_flat[:, None] == jnp.arange(N_EXPERTS, dtype=jnp.int32)[None, :]).astype(jnp.int32)
    counts = jnp.sum(onehot, axis=0)
    rank = jnp.sum((jnp.cumsum(onehot, axis=0) - onehot) * onehot, axis=1)
    padded = (counts + MOE_ROW_TILE - 1) // MOE_ROW_TILE * MOE_ROW_TILE
    pad_end = jnp.cumsum(padded)
    pad_start = pad_end - padded
    dest = (pad_start[e_flat] + rank).astype(jnp.int32)
    n_rows = n * TOP_K + N_EXPERTS * MOE_ROW_TILE
    tok_flat = jnp.repeat(jnp.arange(n, dtype=jnp.int32), TOP_K)
    row_tok = jnp.zeros((n_rows,), jnp.int32).at[dest].set(tok_flat)
    n_blk = n_rows // MOE_ROW_TILE
    blk_expert = jnp.clip(jnp.searchsorted(pad_end, jnp.arange(n_blk, dtype=jnp.int32) * MOE_ROW_TILE, side='right'),
                          0, N_EXPERTS - 1).astype(jnp.int32)
    n_valid = (pad_end[-1] // MOE_ROW_TILE).astype(jnp.int32).reshape(1)
    x_rows = _gather_rows(row_tok, xn)
    y_rows = _experts(blk_expert + layer * N_EXPERTS, n_valid, x_rows, w_gate, w_up, w_down)
    return _combine(dest, y_rows, x, gates)


def _pad_cols(w, width):
    return jnp.pad(w, ((0, 0), (0, width - w.shape[1])))


def _even_layer(x, batch, seq, layer, norm_mix, w_in, w_out, norm_ffn, w_gate, w_up, w_down, tabs64, tabs32):
    qk_b, k_a, k_i, v_b, qa_t, va_t, qi_t, w_t = _even_in(x, norm_mix, w_in, tabs64, tabs32, seq=seq)
    out_a = _dsa(qa_t, k_a, va_t, qi_t, k_i, w_t, batch=batch, seq=seq)
    out_b = _dilated(qk_b.reshape(batch, seq, -1), v_b.reshape(batch, seq, -1), batch=batch, seq=seq)
    x = _out_proj(out_a, out_b.reshape(batch * seq, -1), w_out, x)
    return _ffn(x, norm_ffn, w_gate, w_up, w_down, layer)


def _odd_layer(x, batch, seq, layer, norm_mix, w_in, q_norm, w_uq, kv_norm, w_ukv, conv_w, conv_b, w_a, b_a, w_x, b_x, lam,
               w_out, norm_ffn, w_router, w_gate, w_up, w_down, tabs_mla):
    xg, k_rope, qn_t, qr_t, kn, v_t = _odd_in(x, norm_mix, w_in, q_norm, w_uq, kv_norm, w_ukv, tabs_mla, seq=seq)
    out_c = _mla(qn_t, qr_t, kn, k_rope, v_t, batch=batch, seq=seq)
    out_d = _rglru(xg.reshape(batch, seq, -1), conv_w, conv_b, w_a, b_a, w_x, b_x, lam,
                   batch=batch, seq=seq).reshape(batch * seq, -1)
    x = _out_proj(out_c, out_d, w_out, x)
    return _moe(x, norm_ffn, w_router, w_gate, w_up, w_down, layer)


def kernel(x, ev_norm_mix, ev_w_in, ev_w_out, ev_norm_ffn, ffn_w_gate, ffn_w_up, ffn_w_down, od_norm_mix, od_w_in, mla_q_norm, mla_w_uq, mla_kv_norm, mla_w_ukv, rg_conv_w, rg_conv_b, rg_w_a, rg_b_a, rg_w_x, rg_b_x, rg_lambda, od_w_out, od_norm_ffn, moe_router, moe_w_gate, moe_w_up, moe_w_down, final_norm):
    batch, seq, d = x.shape
    depth = ev_w_in.shape[0] + od_w_in.shape[0]
    assert d == D_MODEL and seq % KEY_CHUNK == 0 and (batch * seq) % FFN_ROW_TILE == 0
    tabs64 = _rope_tables(seq, HEAD_DIM, ROT_DIM)
    tabs32 = _rope_tables(seq, IDX_DIM, IDX_ROT)
    tabs_mla = _rope_tables(seq, C_ROPE, C_ROPE)
    h = x.reshape(batch * seq, d)
    for layer in range(depth):
        i = layer // 2
        if layer % 2 == 0:
            h = _even_layer(h, batch, seq, i, ev_norm_mix[i], ev_w_in[i], ev_w_out[i], ev_norm_ffn[i],
                            ffn_w_gate, ffn_w_up, ffn_w_down, tabs64, tabs32)
        else:
            h = _odd_layer(h, batch, seq, i, od_norm_mix[i], od_w_in[i], mla_q_norm[i], mla_w_uq[i], mla_kv_norm[i],
                           mla_w_ukv[i], rg_conv_w[i], rg_conv_b[i], rg_w_a[i], rg_b_a[i], rg_w_x[i], rg_b_x[i],
                           rg_lambda[i], od_w_out[i], od_norm_ffn[i], moe_router[i], moe_w_gate, moe_w_up,
                           moe_w_down, tabs_mla)
    out = _rmsnorm(h, final_norm, out_dtype=jnp.float32, name="rmsnorm_final")
    return out.reshape(batch, seq, d)
```

```python
import functools

import jax
import jax.numpy as jnp
import numpy as np
from jax import lax
from jax.experimental import pallas as pl
from jax.experimental.pallas import tpu as pltpu

D_MODEL = 1024
HEAD_DIM = 64
ROT_DIM = HEAD_DIM // 4
ROPE_THETA = 500000.0
NORM_EPS = 1e-6

A_HEADS = 8
A_KV_HEADS = 2
IDX_HEADS = 8
IDX_DIM = 32
IDX_ROT = IDX_DIM // 4
TOPK_MAX = 256

B_HEADS = 8
DILATED_PATTERNS = ((128, 1), (512, 4), (2048, 16))

C_HEADS = 8
C_NOPE = 64
C_ROPE = 32
C_V = 64
Q_LORA = 256
KV_LORA = 128

D_RNN = 512
RG_BLOCKS = 8
RG_BW = D_RNN // RG_BLOCKS
CONV_W = 4
RG_C = 8.0

D_FF = 3584
N_EXPERTS = 8
TOP_K = 2

LANES = 128
VMEM_LIMIT_BYTES = 56 * 1024 * 1024
BLOCK_Q = 128
KEY_CHUNK = 512
MLA_BLOCK_Q = 256
DSA_BLOCK_Q = 256
DIL_BLOCK_Q = 256
SUBLANES = 8
LOG2_E = 1.4426950408889634
ROW_TILE = 512
FFN_ROW_TILE = 1024
FFN_COL_TILE = 512
MOE_ROW_TILE = 1024
GATHER_ROWS = 256
DISPATCH_ROWS = 512
RG_CHUNK = 256
NEG_BIG = -1e30
INT_MIN = -2147483648

_NT = (((1,), (1,)), ((), ()))


def _params(semantics, vmem=VMEM_LIMIT_BYTES):
    return pltpu.CompilerParams(dimension_semantics=semantics, vmem_limit_bytes=vmem)


def _bf16(a):
    return a if a.dtype == jnp.bfloat16 else a.astype(jnp.bfloat16)


def _rmsnorm_kernel(x_ref, g_ref, o_ref):
    x = x_ref[...].astype(jnp.float32)
    ms = jnp.mean(x * x, axis=-1, keepdims=True)
    o_ref[...] = (x * lax.rsqrt(ms + NORM_EPS) * g_ref[...]).astype(o_ref.dtype)


def _rmsnorm(x, g, *, out_dtype, name):
    n, width = x.shape
    return pl.pallas_call(
        _rmsnorm_kernel,
        grid=(n // ROW_TILE,),
        in_specs=[pl.BlockSpec((ROW_TILE, width), lambda i: (i, 0)),
                  pl.BlockSpec((1, width), lambda i: (0, 0))],
        out_specs=pl.BlockSpec((ROW_TILE, width), lambda i: (i, 0)),
        out_shape=jax.ShapeDtypeStruct((n, width), out_dtype),
        compiler_params=_params(("parallel",)),
        name=name,
    )(x, g.reshape(1, width).astype(jnp.float32))


def _rope_tables(seq, period, rot_dim):
    half = rot_dim // 2
    inv_freq = 1.0 / (ROPE_THETA ** (jnp.arange(half, dtype=jnp.float32) * (2.0 / rot_dim)))
    ang = jnp.arange(seq, dtype=jnp.float32)[:, None] * inv_freq[None, :]
    cos, sin = jnp.cos(ang), jnp.sin(ang)
    rest = period - rot_dim
    ones = jnp.ones((seq, rest), jnp.float32)
    zeros = jnp.zeros((seq, rest), jnp.float32)
    zh = jnp.zeros((seq, half), jnp.float32)
    c = jnp.concatenate([cos, cos, ones], axis=1)
    sa = jnp.concatenate([-sin, zh, zeros], axis=1)
    sb = jnp.concatenate([zh, sin, zeros], axis=1)
    reps = LANES // period
    return tuple(jnp.tile(t, (1, reps)) for t in (c, sa, sb))


def _out_proj_kernel(a1_ref, a2_ref, w1_ref, w2_ref, res_ref, o_ref):
    y = jnp.dot(_bf16(a1_ref[...]), _bf16(w1_ref[...]), preferred_element_type=jnp.float32)
    y = y + jnp.dot(_bf16(a2_ref[...]), _bf16(w2_ref[...]), preferred_element_type=jnp.float32)
    o_ref[...] = res_ref[...] + y


def _out_proj(a1, a2, w, res):
    n, k1 = a1.shape
    k2 = a2.shape[1]
    d = w.shape[1]
    tm = ROW_TILE
    return pl.pallas_call(
        _out_proj_kernel,
        grid=(n // tm,),
        in_specs=[pl.BlockSpec((tm, k1), lambda i: (i, 0)),
                  pl.BlockSpec((tm, k2), lambda i: (i, 0)),
                  pl.BlockSpec((k1, d), lambda i: (0, 0)),
                  pl.BlockSpec((k2, d), lambda i: (0, 0)),
                  pl.BlockSpec((tm, d), lambda i: (i, 0))],
        out_specs=pl.BlockSpec((tm, d), lambda i: (i, 0)),
        out_shape=jax.ShapeDtypeStruct((n, d), jnp.float32),
        compiler_params=_params(("parallel",)),
        name="proj_out",
    )(a1, a2, w[:k1], w[k1:], res)


def _rms(x, g):
    ms = jnp.mean(x * x, axis=-1, keepdims=True)
    return x * lax.rsqrt(ms + NORM_EPS) * g


def _rope_cols(y, tabs, half):
    width = y.shape[1]
    c, sa, sb = (jnp.tile(t[...], (1, width // LANES)) for t in tabs)
    return y * c + pltpu.roll(y, width - half, 1) * sa + pltpu.roll(y, half, 1) * sb


def _rope_rows(y, tabs_t, half):
    rows = y.shape[0]
    c, sa, sb = (jnp.tile(t[...], (rows // LANES, 1)) for t in tabs_t)
    return y * c + pltpu.roll(y, rows - half, 0) * sa + pltpu.roll(y, half, 0) * sb


def _even_in_kernel(x_ref, g_ref, wkb_ref, wka_ref, wki_ref, wqbt_ref, wvbt_ref, wqat_ref, wvat_ref, wqit_ref, wwt_ref,
                    c64, sa64, sb64, c32, sa32, sb32, c64t, sa64t, sb64t, c32t, sa32t, sb32t,
                    kb_o, ka_o, ki_o, qbt_o, vbt_o, qat_o, vat_o, qit_o, wt_o):
    xn = _rms(x_ref[...], g_ref[...]).astype(jnp.bfloat16)
    nn = lambda w: jnp.dot(xn, w[...], preferred_element_type=jnp.float32)
    nt = lambda wt: lax.dot_general(wt[...], xn, _NT, preferred_element_type=jnp.float32)
    half64, half32 = ROT_DIM // 2, IDX_ROT // 2
    q_scale = HEAD_DIM ** -0.5 * LOG2_E
    kb_o[...] = _rope_cols(nn(wkb_ref), (c64, sa64, sb64), half64).astype(kb_o.dtype)
    ka_o[...] = _rope_cols(nn(wka_ref), (c64, sa64, sb64), half64).astype(ka_o.dtype)
    ki_o[...] = _rope_cols(nn(wki_ref), (c32, sa32, sb32), half32).astype(ki_o.dtype)
    qbt_o[0] = (_rope_rows(nt(wqbt_ref), (c64t, sa64t, sb64t), half64) * q_scale).astype(qbt_o.dtype)
    vbt_o[0] = nt(wvbt_ref).astype(vbt_o.dtype)
    qat_o[0] = (_rope_rows(nt(wqat_ref), (c64t, sa64t, sb64t), half64) * q_scale).astype(qat_o.dtype)
    vat_o[0] = nt(wvat_ref).astype(vat_o.dtype)
    qit_o[0] = _rope_rows(nt(wqit_ref), (c32t, sa32t, sb32t), half32).astype(qit_o.dtype)
    wt_o[0] = nt(wwt_ref) * (IDX_DIM ** -0.5 * IDX_HEADS ** -0.5)


def _resident(shape):
    return pl.BlockSpec(shape, lambda i: (0,) * len(shape))


def _even_in(x, g, w_in, tabs64, tabs32, *, seq):
    n, d = x.shape
    tm = ROW_TILE
    tiles = n // tm
    seq_blocks = seq // tm
    bf = jnp.bfloat16
    qa, ka, va, qi, ki, wi, qb, kb, vb = jnp.split(
        w_in.astype(bf), np.cumsum([512, 128, 128, 256, 32, 8, 512, 512]).tolist(), axis=1)
    weights = [kb, ka, _pad_cols(ki, LANES), qb.T, vb.T, qa.T, va.T, qi.T, wi.T]
    tabs = list(tabs64) + list(tabs32)
    tabs_t = [t.T for t in tabs]
    tok = lambda w: pl.BlockSpec((tm, w), lambda i: (i, 0))
    feat = lambda r: pl.BlockSpec((1, r, tm), lambda i: (i, 0, 0))
    out_shapes = [((n, B_HEADS * HEAD_DIM), bf, tok), ((n, LANES), bf, tok), ((n, LANES), bf, tok),
                  ((tiles, B_HEADS * HEAD_DIM, tm), bf, feat), ((tiles, B_HEADS * HEAD_DIM, tm), bf, feat),
                  ((tiles, A_HEADS * HEAD_DIM, tm), bf, feat), ((tiles, A_KV_HEADS * HEAD_DIM, tm), bf, feat),
                  ((tiles, IDX_HEADS * IDX_DIM, tm), bf, feat), ((tiles, IDX_HEADS, tm), jnp.float32, feat)]
    return pl.pallas_call(
        _even_in_kernel,
        grid=(tiles,),
        in_specs=([tok(d), _resident((1, d))] + [_resident(w.shape) for w in weights]
                  + [pl.BlockSpec((tm, LANES), lambda i: (i % seq_blocks, 0))] * 6
                  + [pl.BlockSpec((LANES, tm), lambda i: (0, i % seq_blocks))] * 6),
        out_specs=[mk(shape[1]) for shape, _, mk in out_shapes],
        out_shape=[jax.ShapeDtypeStruct(shape, dt) for shape, dt, _ in out_shapes],
        compiler_params=_params(("parallel",)),
        name="even_in_proj",
    )(x, g.reshape(1, d).astype(jnp.float32), *weights, *tabs, *tabs_t)


def _odd_in_kernel(x_ref, g_ref, wplain_ref, wkr_ref, qg_ref, kvg_ref, wqnt_ref, wqrt_ref, wkn_ref, wvt_ref,
                   c, sa, sb, ct, sat, sbt, xg_o, kr_o, qnt_o, qrt_o, kn_o, vt_o):
    xn = _rms(x_ref[...], g_ref[...]).astype(jnp.bfloat16)
    po = jnp.dot(xn, wplain_ref[...], preferred_element_type=jnp.float32)
    rnn = 2 * D_RNN
    xg_o[...] = po[:, :rnn]
    half = C_ROPE // 2
    kr = jnp.dot(xn, wkr_ref[...], preferred_element_type=jnp.float32)
    kr_o[...] = _rope_cols(kr, (c, sa, sb), half).astype(kr_o.dtype)
    cqn = _rms(po[:, rnn:rnn + Q_LORA], qg_ref[...]).astype(jnp.bfloat16)
    ckvn = _rms(po[:, rnn + Q_LORA:], kvg_ref[...]).astype(jnp.bfloat16)
    q_scale = (C_NOPE + C_ROPE) ** -0.5 * LOG2_E
    nt = lambda wt, a: lax.dot_general(wt[...], a, _NT, preferred_element_type=jnp.float32)
    qnt_o[0] = (nt(wqnt_ref, cqn) * q_scale).astype(qnt_o.dtype)
    qrt_o[0] = (_rope_rows(nt(wqrt_ref, cqn), (ct, sat, sbt), half) * q_scale).astype(qrt_o.dtype)
    kn_o[...] = jnp.dot(ckvn, wkn_ref[...], preferred_element_type=jnp.float32).astype(kn_o.dtype)
    vt_o[0] = nt(wvt_ref, ckvn).astype(vt_o.dtype)


def _odd_in(x, g, w_in, q_norm, w_uq, kv_norm, w_ukv, tabs_mla, *, seq):
    n, d = x.shape
    tm = ROW_TILE
    tiles = n // tm
    seq_blocks = seq // tm
    bf = jnp.bfloat16
    cq, ckv, kr, xr, gr = jnp.split(w_in.astype(bf), np.cumsum([Q_LORA, KV_LORA, C_ROPE, D_RNN]).tolist(), axis=1)
    w_uq = w_uq.astype(bf).reshape(Q_LORA, C_HEADS, C_NOPE + C_ROPE)
    w_ukv = w_ukv.astype(bf).reshape(KV_LORA, C_HEADS, C_NOPE + C_V)
    weights = [jnp.concatenate([xr, gr, cq, ckv], axis=1), _pad_cols(kr, LANES),
               q_norm.reshape(1, Q_LORA).astype(jnp.float32), kv_norm.reshape(1, KV_LORA).astype(jnp.float32),
               w_uq[:, :, :C_NOPE].reshape(Q_LORA, C_HEADS * C_NOPE).T,
               w_uq[:, :, C_NOPE:].reshape(Q_LORA, C_HEADS * C_ROPE).T,
               w_ukv[:, :, :C_NOPE].reshape(KV_LORA, C_HEADS * C_NOPE),
               w_ukv[:, :, C_NOPE:].reshape(KV_LORA, C_HEADS * C_V).T]
    tabs = list(tabs_mla)
    tabs_t = [t.T for t in tabs]
    tok = lambda w: pl.BlockSpec((tm, w), lambda i: (i, 0))
    feat = lambda r: pl.BlockSpec((1, r, tm), lambda i: (i, 0, 0))
    out_shapes = [((n, 2 * D_RNN), jnp.float32, tok), ((n, LANES), bf, tok),
                  ((tiles, C_HEADS * C_NOPE, tm), bf, feat), ((tiles, C_HEADS * C_ROPE, tm), bf, feat),
                  ((n, C_HEADS * C_NOPE), bf, tok), ((tiles, C_HEADS * C_V, tm), bf, feat)]
    return pl.pallas_call(
        _odd_in_kernel,
        grid=(tiles,),
        in_specs=([tok(d), _resident((1, d))] + [_resident(w.shape) for w in weights]
                  + [pl.BlockSpec((tm, LANES), lambda i: (i % seq_blocks, 0))] * 3
                  + [pl.BlockSpec((LANES, tm), lambda i: (0, i % seq_blocks))] * 3),
        out_specs=[mk(shape[1]) for shape, _, mk in out_shapes],
        out_shape=[jax.ShapeDtypeStruct(shape, dt) for shape, dt, _ in out_shapes],
        compiler_params=_params(("parallel",)),
        name="odd_in_proj",
    )(x, g.reshape(1, d).astype(jnp.float32), *weights, *tabs, *tabs_t)


def _silu(g):
    return g * (1.0 / (1.0 + jnp.exp(-g)))


def _ffn_kernel(x_ref, g_ref, wg_ref, wu_ref, wd_ref, o_ref, acc_ref, xn_ref):
    f = pl.program_id(1)

    @pl.when(f == 0)
    def _():
        acc_ref[...] = jnp.zeros_like(acc_ref)
        xn_ref[...] = _rms(x_ref[...], g_ref[...]).astype(xn_ref.dtype)

    xn = xn_ref[...]
    g = jnp.dot(xn, _bf16(wg_ref[0]), preferred_element_type=jnp.float32)
    u = jnp.dot(xn, _bf16(wu_ref[0]), preferred_element_type=jnp.float32)
    h = (_silu(g) * u).astype(jnp.bfloat16)
    acc_ref[...] += jnp.dot(h, _bf16(wd_ref[0]), preferred_element_type=jnp.float32)

    @pl.when(f == pl.num_programs(1) - 1)
    def _():
        o_ref[...] = x_ref[...] + acc_ref[...]


def _ffn(x, g, w_gate, w_up, w_down, layer):
    n, d = x.shape
    ff = w_gate.shape[2]
    tm, tf = FFN_ROW_TILE, FFN_COL_TILE
    return pl.pallas_call(
        _ffn_kernel,
        grid=(n // tm, ff // tf),
        in_specs=[pl.BlockSpec((tm, d), lambda i, f: (i, 0)),
                  pl.BlockSpec((1, d), lambda i, f: (0, 0)),
                  pl.BlockSpec((1, d, tf), lambda i, f: (layer, 0, f)),
                  pl.BlockSpec((1, d, tf), lambda i, f: (layer, 0, f)),
                  pl.BlockSpec((1, tf, d), lambda i, f: (layer, f, 0))],
        out_specs=pl.BlockSpec((tm, d), lambda i, f: (i, 0)),
        out_shape=jax.ShapeDtypeStruct((n, d), jnp.float32),
        scratch_shapes=[pltpu.VMEM((tm, d), jnp.float32), pltpu.VMEM((tm, d), jnp.bfloat16)],
        compiler_params=_params(("parallel", "arbitrary")),
        name="ffn_swiglu",
    )(x, g.reshape(1, d).astype(jnp.float32), w_gate, w_up, w_down)


def _dsa_kernel(qt_ref, k_ref, vt_ref, qit_ref, ki_ref, wt_ref, o_ref, key_scr, bias_scr, *scr, topk):
    acc_scr, m_scr, l_scr = scr[:A_HEADS], scr[A_HEADS:2 * A_HEADS], scr[2 * A_HEADS:]
    i = pl.program_id(1)
    tq = DSA_BLOCK_Q
    n_chunks = (i * tq + tq + KEY_CHUNK - 1) // KEY_CHUNK
    qpos = i * tq + lax.broadcasted_iota(jnp.int32, (1, tq), 1)
    kloc = lax.broadcasted_iota(jnp.int32, (KEY_CHUNK, 1), 0)

    w = wt_ref[0]

    def index_body(c, carry):
        kc = ki_ref[pl.ds(c * KEY_CHUNK, KEY_CHUNK), :][:, :IDX_DIM]
        score = jnp.zeros((KEY_CHUNK, tq), jnp.float32)
        for h in range(IDX_HEADS):
            lg = jnp.dot(kc, qit_ref[0, h * IDX_DIM:(h + 1) * IDX_DIM, :], preferred_element_type=jnp.float32)
            score = score + w[h:h + 1, :] * jnp.maximum(lg, 0.0)
        bits = pltpu.bitcast(score, jnp.int32)
        key = jnp.where(bits < 0, bits ^ jnp.int32(0x7FFFFFFF), bits)
        key_scr[c] = jnp.where(c * KEY_CHUNK + kloc <= qpos, key, jnp.int32(INT_MIN))
        return carry

    lax.fori_loop(0, n_chunks, index_body, 0)

    def count(pred):
        def body(c, acc):
            hit = pred(key_scr[c]).astype(jnp.int32)
            return acc + jnp.sum(hit.reshape(KEY_CHUNK // SUBLANES, SUBLANES, tq), axis=0)
        acc = lax.fori_loop(0, n_chunks, body, jnp.zeros((SUBLANES, tq), jnp.int32))
        return jnp.sum(acc, axis=0, keepdims=True)

    def bit_body(t, res):
        cand = res | jnp.left_shift(jnp.int32(1), 31 - t)
        cand_key = cand ^ jnp.int32(INT_MIN)
        return jnp.where(count(lambda key: key >= cand_key) >= topk, cand, res)

    thr = lax.fori_loop(0, 32, bit_body, jnp.zeros((1, tq), jnp.int32)) ^ jnp.int32(INT_MIN)
    need = (topk - count(lambda key: key > thr)).astype(jnp.float32)

    tri = (lax.broadcasted_iota(jnp.int32, (KEY_CHUNK, KEY_CHUNK), 1)
           <= lax.broadcasted_iota(jnp.int32, (KEY_CHUNK, KEY_CHUNK), 0)).astype(jnp.bfloat16)

    def mask_body(c, seen):
        key = key_scr[c]
        eq = key == thr
        running = jnp.dot(tri, eq.astype(jnp.bfloat16), preferred_element_type=jnp.float32) + seen
        sel = ((key > thr) | (eq & (running <= need))) & (c * KEY_CHUNK + kloc <= qpos)
        bias_scr[c] = jnp.where(sel, 0.0, NEG_BIG)
        return running[KEY_CHUNK - 1:KEY_CHUNK, :]

    lax.fori_loop(0, n_chunks, mask_body, jnp.zeros((1, tq), jnp.float32))

    for h in range(A_HEADS):
        acc_scr[h][...] = jnp.zeros_like(acc_scr[h])
        l_scr[h][...] = jnp.zeros_like(l_scr[h])
        m_scr[h][...] = jnp.full_like(m_scr[h], NEG_BIG)
    group = A_HEADS // A_KV_HEADS

    def att_body(c, carry):
        rows = pl.ds(c * KEY_CHUNK, KEY_CHUNK)
        bias = bias_scr[c]
        scores = []
        for h in range(A_HEADS):
            n = h // group
            scores.append(jnp.dot(k_ref[rows, n * HEAD_DIM:(n + 1) * HEAD_DIM],
                                  qt_ref[0, h * HEAD_DIM:(h + 1) * HEAD_DIM, :],
                                  preferred_element_type=jnp.float32))
        probs, alphas = [], []
        for h in range(A_HEADS):
            s = scores[h] + bias
            m_old = m_scr[h][...]
            m_new = jnp.maximum(m_old, jnp.max(s, axis=0, keepdims=True))
            alpha = jnp.exp2(m_old - m_new)
            p = jnp.exp2(s - m_new)
            l_scr[h][...] = l_scr[h][...] * alpha + jnp.sum(p, axis=0, keepdims=True)
            m_scr[h][...] = m_new
            probs.append(p.astype(jnp.bfloat16))
            alphas.append(alpha)
        for h in range(A_HEADS):
            n = h // group
            acc_scr[h][...] = acc_scr[h][...] * alphas[h] + jnp.dot(
                vt_ref[c, n * HEAD_DIM:(n + 1) * HEAD_DIM, :], probs[h], preferred_element_type=jnp.float32)
        return carry

    lax.fori_loop(0, n_chunks, att_body, 0)
    out_t = jnp.concatenate([acc_scr[h][...] * (1.0 / l_scr[h][...]) for h in range(A_HEADS)], axis=0)
    o_ref[...] = out_t.T


def _dsa(qa_t, ka, va_t, qi_t, ki, w_t, *, batch, seq):
    topk = min(TOPK_MAX, seq // 4)
    tq = DSA_BLOCK_Q
    tiles = seq // ROW_TILE
    per_tile = ROW_TILE // tq
    n_chunks = seq // KEY_CHUNK
    qw = A_HEADS * HEAD_DIM
    kvw = A_KV_HEADS * HEAD_DIM
    q_tile = lambda b, i: (b * tiles + i // per_tile, 0, i % per_tile)
    return pl.pallas_call(
        functools.partial(_dsa_kernel, topk=topk),
        grid=(batch, seq // tq),
        in_specs=[pl.BlockSpec((1, qw, tq), q_tile),
                  pl.BlockSpec((seq, kvw), lambda b, i: (b, 0)),
                  pl.BlockSpec((tiles, kvw, ROW_TILE), lambda b, i: (b, 0, 0)),
                  pl.BlockSpec((1, IDX_HEADS * IDX_DIM, tq), q_tile),
                  pl.BlockSpec((seq, LANES), lambda b, i: (b, 0)),
                  pl.BlockSpec((1, IDX_HEADS, tq), q_tile)],
        out_specs=pl.BlockSpec((tq, qw), lambda b, i: (b * (seq // tq) + i, 0)),
        out_shape=jax.ShapeDtypeStruct((batch * seq, qw), jnp.float32),
        scratch_shapes=([pltpu.VMEM((n_chunks, KEY_CHUNK, tq), jnp.int32),
                         pltpu.VMEM((n_chunks, KEY_CHUNK, tq), jnp.float32)]
                        + [pltpu.VMEM((HEAD_DIM, tq), jnp.float32)] * A_HEADS
                        + [pltpu.VMEM((1, tq), jnp.float32)] * (2 * A_HEADS)),
        compiler_params=_params(("parallel", "arbitrary")),
        name="dsa_attention",
    )(qa_t, ka, va_t, qi_t, ki, w_t)


def _dilated_kernel(qt_ref, k_ref, vt_ref, o_ref, *scr):
    acc_scr, m_scr, l_scr = scr[:B_HEADS], scr[B_HEADS:2 * B_HEADS], scr[2 * B_HEADS:]
    i = pl.program_id(1)
    tq = DIL_BLOCK_Q
    n_chunks = (i * tq + tq + KEY_CHUNK - 1) // KEY_CHUNK
    qpos = i * tq + lax.broadcasted_iota(jnp.int32, (1, tq), 1)
    kloc = lax.broadcasted_iota(jnp.int32, (KEY_CHUNK, 1), 0)

    for h in range(B_HEADS):
        acc_scr[h][...] = jnp.zeros_like(acc_scr[h])
        l_scr[h][...] = jnp.zeros_like(l_scr[h])
        m_scr[h][...] = jnp.full_like(m_scr[h], NEG_BIG)

    def chunk(c, carry):
        rows = pl.ds(c * KEY_CHUNK, KEY_CHUNK)
        delta = qpos - (c * KEY_CHUNK + kloc)
        mult = jnp.zeros((KEY_CHUNK, tq), jnp.float32)
        for (window, dil) in DILATED_PATTERNS:
            assert dil & (dil - 1) == 0
            hit = (delta >= 0) & (delta <= window) & ((delta & (dil - 1)) == 0)
            mult = mult + jnp.where(hit, 1.0, 0.0)
        valid = mult > 0.0
        scores = [jnp.dot(k_ref[rows, h * HEAD_DIM:(h + 1) * HEAD_DIM], qt_ref[0, h * HEAD_DIM:(h + 1) * HEAD_DIM, :],
                          preferred_element_type=jnp.float32) for h in range(B_HEADS)]
        probs, alphas = [], []
        for h in range(B_HEADS):
            s = jnp.where(valid, scores[h], NEG_BIG)
            m_old = m_scr[h][...]
            m_new = jnp.maximum(m_old, jnp.max(s, axis=0, keepdims=True))
            alpha = jnp.exp2(m_old - m_new)
            p = jnp.exp2(s - m_new) * mult
            l_scr[h][...] = l_scr[h][...] * alpha + jnp.sum(p, axis=0, keepdims=True)
            m_scr[h][...] = m_new
            probs.append(p.astype(jnp.bfloat16))
            alphas.append(alpha)
        for h in range(B_HEADS):
            hv = slice(h * HEAD_DIM, (h + 1) * HEAD_DIM)
            acc_scr[h][...] = acc_scr[h][...] * alphas[h] + jnp.dot(vt_ref[c, hv, :], probs[h],
                                                                     preferred_element_type=jnp.float32)
        return carry

    lax.fori_loop(0, n_chunks, chunk, 0)
    out_t = jnp.concatenate([acc_scr[h][...] * (1.0 / l_scr[h][...]) for h in range(B_HEADS)], axis=0)
    o_ref[...] = out_t.T


def _dilated(q_t, k, v_t, *, batch, seq):
    tq = DIL_BLOCK_Q
    tiles = seq // ROW_TILE
    per_tile = ROW_TILE // tq
    width = B_HEADS * HEAD_DIM
    assert all(w <= seq for w, _ in DILATED_PATTERNS)
    return pl.pallas_call(
        _dilated_kernel,
        grid=(batch, seq // tq),
        in_specs=[pl.BlockSpec((1, width, tq), lambda b, i: (b * tiles + i // per_tile, 0, i % per_tile)),
                  pl.BlockSpec((seq, width), lambda b, i: (b, 0)),
                  pl.BlockSpec((tiles, width, ROW_TILE), lambda b, i: (b, 0, 0))],
        out_specs=pl.BlockSpec((tq, width), lambda b, i: (b * (seq // tq) + i, 0)),
        out_shape=jax.ShapeDtypeStruct((batch * seq, width), jnp.float32),
        scratch_shapes=([pltpu.VMEM((HEAD_DIM, tq), jnp.float32)] * B_HEADS
                        + [pltpu.VMEM((1, tq), jnp.float32)] * (2 * B_HEADS)),
        compiler_params=_params(("parallel", "arbitrary")),
        name="dilated_attention",
    )(q_t, k, v_t)


def _dilated_strided_kernel(*refs, seq):
    pairs = B_HEADS // 2
    q_refs, k_refs, v_refs = refs[:pairs], refs[pairs:2 * pairs], refs[2 * pairs:3 * pairs]
    o_ref = refs[3 * pairs]
    scr = refs[3 * pairs + 1:]
    acc_scr, m_scr, l_scr = scr[:pairs], scr[pairs:2 * pairs], scr[2 * pairs:]
    scale = HEAD_DIM ** -0.5
    row = lax.broadcasted_iota(jnp.int32, (BLOCK_Q, BLOCK_Q), 0)
    col = lax.broadcasted_iota(jnp.int32, (BLOCK_Q, BLOCK_Q), 1)
    mask_cur = col <= row
    mask_prev = col >= row

    for p in range(pairs):
        acc_scr[p][...] = jnp.zeros_like(acc_scr[p])
        l_scr[p][...] = jnp.zeros_like(l_scr[p])
        m_scr[p][...] = jnp.full_like(m_scr[p], NEG_BIG)

    def gather(ref_list, rows):
        return jnp.concatenate([_bf16(r[0, rows, :]) for r in ref_list], axis=1)

    def spread(cols):
        return jnp.concatenate([jnp.broadcast_to(c, (BLOCK_Q, HEAD_DIM)) for c in cols], axis=1)

    for (window, dil) in DILATED_PATTERNS:
        assert window // dil == BLOCK_Q and seq % (dil * BLOCK_Q) == 0
        n_blocks = seq // (dil * BLOCK_Q)

        def block_body(idx, carry, dil=dil, n_blocks=n_blocks):
            phase = idx // n_blocks
            blk = idx % n_blocks
            def view_rows(block):
                if dil == 1:
                    return pl.ds(pl.multiple_of(BLOCK_Q * block, BLOCK_Q), BLOCK_Q)
                return pl.ds(phase + dil * BLOCK_Q * block, BLOCK_Q, stride=dil)

            rows = view_rows(blk)
            q2 = gather(q_refs, rows)
            k_cur = gather(k_refs, rows)
            v_cur = gather(v_refs, rows)
            if n_blocks > 1:
                rows_p = view_rows(jnp.maximum(blk - 1, 0))
                k_prev = gather(k_refs, rows_p)
                v_prev = gather(v_refs, rows_p)
                mask_p = mask_prev & (blk > 0)
            m_old = jnp.concatenate([m[rows, :] for m in m_scr], axis=1)
            heads = q2.shape[1] // HEAD_DIM
            lanes = [slice(h * HEAD_DIM, (h + 1) * HEAD_DIM) for h in range(heads)]
            s_cur = [lax.dot_general(q2[:, sl], k_cur[:, sl], _NT, preferred_element_type=jnp.float32) for sl in lanes]
            if n_blocks > 1:
                s_prev = [lax.dot_general(q2[:, sl], k_prev[:, sl], _NT, preferred_element_type=jnp.float32)
                          for sl in lanes]
            new_m, new_sum, p_cur, p_prev = [], [], [], []
            for h in range(heads):
                s_c = jnp.where(mask_cur, s_cur[h] * scale, NEG_BIG)
                mx = jnp.max(s_c, axis=1, keepdims=True)
                if n_blocks > 1:
                    s_p = jnp.where(mask_p, s_prev[h] * scale, NEG_BIG)
                    mx = jnp.maximum(mx, jnp.max(s_p, axis=1, keepdims=True))
                m_h = jnp.maximum(m_old[:, h * HEAD_DIM:h * HEAD_DIM + 1], mx)
                p_c = jnp.exp(s_c - m_h)
                psum = jnp.sum(p_c, axis=1, keepdims=True)
                p_cur.append(p_c.astype(jnp.bfloat16))
                if n_blocks > 1:
                    p_p = jnp.exp(s_p - m_h)
                    psum = psum + jnp.sum(p_p, axis=1, keepdims=True)
                    p_prev.append(p_p.astype(jnp.bfloat16))
                new_m.append(m_h)
                new_sum.append(psum)
            new_pv = []
            for h in range(heads):
                pv = jnp.dot(p_cur[h], v_cur[:, lanes[h]], preferred_element_type=jnp.float32)
                if n_blocks > 1:
                    pv = pv + jnp.dot(p_prev[h], v_prev[:, lanes[h]], preferred_element_type=jnp.float32)
                new_pv.append(pv)
            m_new = spread(new_m)
            alpha = jnp.exp(m_old - m_new)
            sums = spread(new_sum)
            pvs = jnp.concatenate(new_pv, axis=1)
            for p in range(pairs):
                cols = slice(p * LANES, (p + 1) * LANES)
                m_scr[p][rows, :] = m_new[:, cols]
                l_scr[p][rows, :] = l_scr[p][rows, :] * alpha[:, cols] + sums[:, cols]
                acc_scr[p][rows, :] = acc_scr[p][rows, :] * alpha[:, cols] + pvs[:, cols]
            return carry

        lax.fori_loop(0, dil * n_blocks, block_body, 0)

    for p in range(pairs):
        o_ref[0, :, p * LANES:(p + 1) * LANES] = acc_scr[p][...] * (1.0 / l_scr[p][...])


def _dilated_strided(qk, v, *, batch, seq):
    width = B_HEADS * HEAD_DIM
    pairs = B_HEADS // 2

    def col_block(j):
        return pl.BlockSpec((1, seq, LANES), lambda b: (b, 0, j))

    return pl.pallas_call(
        functools.partial(_dilated_strided_kernel, seq=seq),
        grid=(batch,),
        in_specs=([col_block(p) for p in range(pairs)] + [col_block(pairs + p) for p in range(pairs)]
                  + [col_block(p) for p in range(pairs)]),
        out_specs=pl.BlockSpec((1, seq, width), lambda b: (b, 0, 0)),
        out_shape=jax.ShapeDtypeStruct((batch, seq, width), jnp.float32),
        scratch_shapes=[pltpu.VMEM((seq, LANES), jnp.float32)] * (3 * pairs),
        compiler_params=_params(("parallel",)),
        name="dilated_attention",
    )(*([qk] * (2 * pairs) + [v] * pairs))


def _mla_kernel(qn_ref, qr_ref, kn_ref, kr_ref, vt_ref, o_ref, *scr):
    acc_scr, m_scr, l_scr = scr[:C_HEADS], scr[C_HEADS:2 * C_HEADS], scr[2 * C_HEADS:]
    i = pl.program_id(1)
    tq = MLA_BLOCK_Q
    n_full = (i * tq + 1) // KEY_CHUNK
    n_all = (i * tq + tq + KEY_CHUNK - 1) // KEY_CHUNK
    qpos = i * tq + lax.broadcasted_iota(jnp.int32, (1, tq), 1)
    kloc = lax.broadcasted_iota(jnp.int32, (KEY_CHUNK, 1), 0)

    for h in range(C_HEADS):
        acc_scr[h][...] = jnp.zeros_like(acc_scr[h])
        l_scr[h][...] = jnp.zeros_like(l_scr[h])
        m_scr[h][...] = jnp.full_like(m_scr[h], NEG_BIG)

    def chunk(c, carry, masked):
        rows = pl.ds(c * KEY_CHUNK, KEY_CHUNK)
        kr = kr_ref[rows, :][:, :C_ROPE]
        visible = (c * KEY_CHUNK + kloc) <= qpos
        scores = []
        for h in range(C_HEADS):
            hn = slice(h * C_NOPE, (h + 1) * C_NOPE)
            scores.append(jnp.dot(kn_ref[rows, hn], qn_ref[0, hn, :], preferred_element_type=jnp.float32)
                          + jnp.dot(kr, qr_ref[0, h * C_ROPE:(h + 1) * C_ROPE, :],
                                    preferred_element_type=jnp.float32))
        probs, alphas = [], []
        for h in range(C_HEADS):
            s = jnp.where(visible, scores[h], NEG_BIG) if masked else scores[h]
            m_old = m_scr[h][...]
            m_new = jnp.maximum(m_old, jnp.max(s, axis=0, keepdims=True))
            alpha = jnp.exp2(m_old - m_new)
            p = jnp.exp2(s - m_new)
            l_scr[h][...] = l_scr[h][...] * alpha + jnp.sum(p, axis=0, keepdims=True)
            m_scr[h][...] = m_new
            probs.append(p.astype(jnp.bfloat16))
            alphas.append(alpha)
        for h in range(C_HEADS):
            hv = slice(h * C_V, (h + 1) * C_V)
            acc_scr[h][...] = acc_scr[h][...] * alphas[h] + jnp.dot(vt_ref[c, hv, :], probs[h],
                                                                     preferred_element_type=jnp.float32)
        return carry

    lax.fori_loop(0, n_full, functools.partial(chunk, masked=False), 0)
    lax.fori_loop(n_full, n_all, functools.partial(chunk, masked=True), 0)

    out_t = jnp.concatenate([acc_scr[h][...] * (1.0 / l_scr[h][...]) for h in range(C_HEADS)], axis=0)
    o_ref[...] = out_t.T


def _mla(qn_t, qr_t, kn, kr, v_t, *, batch, seq):
    tq = MLA_BLOCK_Q
    tiles = seq // ROW_TILE
    per_tile = ROW_TILE // tq
    kw = C_HEADS * C_NOPE
    vw = C_HEADS * C_V
    n = batch * seq
    return pl.pallas_call(
        _mla_kernel,
        grid=(batch, seq // tq),
        in_specs=[pl.BlockSpec((1, kw, tq), lambda b, i: (b * tiles + i // per_tile, 0, i % per_tile)),
                  pl.BlockSpec((1, C_HEADS * C_ROPE, tq), lambda b, i: (b * tiles + i // per_tile, 0, i % per_tile)),
                  pl.BlockSpec((seq, kw), lambda b, i: (b, 0)),
                  pl.BlockSpec((seq, LANES), lambda b, i: (b, 0)),
                  pl.BlockSpec((tiles, vw, ROW_TILE), lambda b, i: (b, 0, 0))],
        out_specs=pl.BlockSpec((tq, vw), lambda b, i: (b * (seq // tq) + i, 0)),
        out_shape=jax.ShapeDtypeStruct((n, vw), jnp.float32),
        scratch_shapes=([pltpu.VMEM((C_V, tq), jnp.float32)] * C_HEADS
                        + [pltpu.VMEM((1, tq), jnp.float32)] * (2 * C_HEADS)),
        compiler_params=_params(("parallel", "arbitrary")),
        name="mla_attention",
    )(qn_t, qr_t, kn, kr, v_t)


def _expm1(y):
    u = jnp.exp(y)
    safe = jnp.where(u == 1.0, 2.0, u)
    return jnp.where(u == 1.0, y, (u - 1.0) * y / jnp.log(safe))


def _gelu_tanh(x):
    return 0.5 * x * (1.0 + jnp.tanh(np.sqrt(2.0 / np.pi).astype(np.float32) * (x + 0.044715 * (x * x * x))))


def _rglru_kernel(x_ref, g_ref, cw_ref, cb_ref, wa_ref, ba_ref, wx_ref, bx_ref, lam_ref, o_ref, a_scr, u_scr, *, seq):
    cw = cw_ref[...]
    neg_c_softplus = -RG_C * jnp.log1p(jnp.exp(-jnp.abs(-lam_ref[...]))) - RG_C * jnp.maximum(-lam_ref[...], 0.0)
    hi = lax.Precision.HIGHEST
    sub = 8

    def chunk_body(ci, h):
        r0 = pl.multiple_of(ci * RG_CHUNK, RG_CHUNK)
        xa = x_ref[0, pl.ds(r0, RG_CHUNK), :]
        prev = x_ref[0, pl.ds(pl.multiple_of(jnp.maximum(r0 - sub, 0), sub), sub), :]
        prev = jnp.where(ci > 0, prev, 0.0)
        xcat = jnp.concatenate([prev, xa], axis=0)
        xc = cw[CONV_W - 1:CONV_W, :] * xa + cb_ref[...]
        for j in range(1, CONV_W):
            xc = xc + cw[CONV_W - 1 - j:CONV_W - j, :] * pltpu.roll(xcat, j, 0)[sub:, :]
        r = 1.0 / (1.0 + jnp.exp(-(jnp.dot(xc, wa_ref[...], precision=hi, preferred_element_type=jnp.float32)
                                   + ba_ref[...])))
        ig = 1.0 / (1.0 + jnp.exp(-(jnp.dot(xc, wx_ref[...], precision=hi, preferred_element_type=jnp.float32)
                                    + bx_ref[...])))
        log_a = neg_c_softplus * r
        a_scr[...] = jnp.exp(log_a)
        u_scr[...] = jnp.sqrt(-_expm1(2.0 * log_a)) * (ig * xc)
        gate = _gelu_tanh(g_ref[0, pl.ds(r0, RG_CHUNK), :])

        def group_body(gi, h):
            g0 = pl.multiple_of(gi * sub, sub)
            a8 = a_scr[pl.ds(g0, sub), :]
            u8 = u_scr[pl.ds(g0, sub), :]
            rows = []
            for t in range(sub):
                h = a8[t:t + 1, :] * h + u8[t:t + 1, :]
                rows.append(h)
            u_scr[pl.ds(g0, sub), :] = jnp.concatenate(rows, axis=0)
            return h

        h = lax.fori_loop(0, RG_CHUNK // sub, group_body, h)
        o_ref[0, pl.ds(r0, RG_CHUNK), :] = u_scr[...] * gate
        return h

    lax.fori_loop(0, seq // RG_CHUNK, chunk_body, jnp.zeros((1, D_RNN), jnp.float32))


def _block_diag(w):
    nb, bw, _ = w.shape
    eye = jnp.eye(nb, dtype=w.dtype)
    return (eye[:, None, :, None] * w[:, :, None, :]).reshape(nb * bw, nb * bw)


def _rglru(po, conv_w, conv_b, w_a, b_a, w_x, b_x, lam, *, batch, seq):
    row = lambda a: a.reshape(1, D_RNN).astype(jnp.float32)
    vec = pl.BlockSpec((1, D_RNN), lambda b: (0, 0))
    mat = pl.BlockSpec((D_RNN, D_RNN), lambda b: (0, 0))
    return pl.pallas_call(
        functools.partial(_rglru_kernel, seq=seq),
        grid=(batch,),
        in_specs=[pl.BlockSpec((1, seq, D_RNN), lambda b: (b, 0, 0)),
                  pl.BlockSpec((1, seq, D_RNN), lambda b: (b, 0, 1)),
                  pl.BlockSpec((CONV_W, D_RNN), lambda b: (0, 0)),
                  vec, mat, vec, mat, vec, vec],
        out_specs=pl.BlockSpec((1, seq, D_RNN), lambda b: (b, 0, 0)),
        out_shape=jax.ShapeDtypeStruct((batch, seq, D_RNN), jnp.float32),
        scratch_shapes=[pltpu.VMEM((RG_CHUNK, D_RNN), jnp.float32)] * 2,
        compiler_params=_params(("parallel",)),
        name="rglru",
    )(po, po, conv_w.astype(jnp.float32), row(conv_b), _block_diag(w_a), row(b_a), _block_diag(w_x), row(b_x), row(lam))


def _pack_bf16_pairs(x):
    w = x.shape[1] // 2
    hi = pltpu.bitcast(x[:, :w].astype(jnp.bfloat16).astype(jnp.float32), jnp.uint32)
    lo = pltpu.bitcast(x[:, w:].astype(jnp.bfloat16).astype(jnp.float32), jnp.uint32)
    return hi | (lo >> 16)


def _unpack_bf16_pairs(p):
    hi = pltpu.bitcast(p & jnp.uint32(0xFFFF0000), jnp.float32)
    lo = pltpu.bitcast(p << 16, jnp.float32)
    return jnp.concatenate([hi, lo], axis=1).astype(jnp.bfloat16)


def _router_kernel(x_ref, g_ref, wr_ref, xn_ref, idx_ref, gate_ref):
    x = x_ref[...]
    ms = jnp.mean(x * x, axis=-1, keepdims=True)
    xn = x * lax.rsqrt(ms + NORM_EPS) * g_ref[...]
    xn_ref[...] = _pack_bf16_pairs(xn)
    logits = jnp.dot(xn, wr_ref[...], precision=lax.Precision.HIGHEST, preferred_element_type=jnp.float32)
    lane = lax.broadcasted_iota(jnp.int32, logits.shape, 1)
    logits = jnp.where(lane < N_EXPERTS, logits, -jnp.inf)
    m1 = jnp.max(logits, axis=1, keepdims=True)
    i1 = jnp.min(jnp.where(logits == m1, lane, LANES), axis=1, keepdims=True)
    rest = jnp.where(lane == i1, -jnp.inf, logits)
    m2 = jnp.max(rest, axis=1, keepdims=True)
    i2 = jnp.min(jnp.where(rest == m2, lane, LANES), axis=1, keepdims=True)
    e = jnp.exp(m2 - m1)
    g1 = 1.0 / (1.0 + e)
    g2 = e / (1.0 + e)
    idx_ref[...] = jnp.where(lane == 0, i1, jnp.where(lane == 1, i2, 0))
    gate_ref[...] = jnp.where(lane == 0, g1, jnp.where(lane == 1, g2, 0.0))


def _router(x, g, w_router):
    n, d = x.shape
    wr = jnp.pad(w_router.astype(jnp.float32), ((0, 0), (0, LANES - N_EXPERTS)))
    tm = ROW_TILE
    return pl.pallas_call(
        _router_kernel,
        grid=(n // tm,),
        in_specs=[pl.BlockSpec((tm, d), lambda i: (i, 0)),
                  pl.BlockSpec((1, d), lambda i: (0, 0)),
                  pl.BlockSpec((d, LANES), lambda i: (0, 0))],
        out_specs=[pl.BlockSpec((tm, d // 2), lambda i: (i, 0)),
                   pl.BlockSpec((tm, LANES), lambda i: (i, 0)),
                   pl.BlockSpec((tm, LANES), lambda i: (i, 0))],
        out_shape=[jax.ShapeDtypeStruct((n, d // 2), jnp.uint32),
                   jax.ShapeDtypeStruct((n, LANES), jnp.int32),
                   jax.ShapeDtypeStruct((n, LANES), jnp.float32)],
        compiler_params=_params(("parallel",)),
        name="moe_router",
    )(x, g.reshape(1, d).astype(jnp.float32), wr)


def _row_copy(src_hbm, dst, src_row, dst_row, sem):
    return pltpu.make_async_copy(src_hbm.at[pl.ds(src_row, 1)], dst.at[pl.ds(dst_row, 1)], sem)


def _gather_kernel(tok_ref, x_hbm, o_ref, x_vmem, sem):
    @pl.when(pl.program_id(0) == 0)
    def _():
        copy = pltpu.make_async_copy(x_hbm, x_vmem, sem)
        copy.start()
        copy.wait()

    base = pl.program_id(0) * DISPATCH_ROWS

    def group(g, carry):
        r0 = pl.multiple_of(g * SUBLANES, SUBLANES)
        rows = [x_vmem[tok_ref[base + r0 + j]] for j in range(SUBLANES)]
        o_ref[pl.ds(r0, SUBLANES), :] = jnp.concatenate(rows, axis=0)
        return carry

    lax.fori_loop(0, DISPATCH_ROWS // SUBLANES, group, 0)


def _gather_rows(row_tok, x):
    n_rows = row_tok.shape[0]
    n, w = x.shape
    return pl.pallas_call(
        _gather_kernel,
        grid_spec=pltpu.PrefetchScalarGridSpec(
            num_scalar_prefetch=1,
            grid=(n_rows // DISPATCH_ROWS,),
            in_specs=[pl.BlockSpec(memory_space=pl.ANY)],
            out_specs=pl.BlockSpec((DISPATCH_ROWS, w), lambda i, tok: (i, 0)),
            scratch_shapes=[pltpu.VMEM((n, 1, w), x.dtype), pltpu.SemaphoreType.DMA(())]),
        out_shape=jax.ShapeDtypeStruct((n_rows, w), x.dtype),
        compiler_params=_params(("arbitrary",)),
        name="moe_gather",
    )(row_tok, x.reshape(n, 1, w))


def _experts_kernel(be_ref, nv_ref, x_ref, wg_ref, wu_ref, wd_ref, o_ref, acc_ref, xb_ref):
    i = pl.program_id(0)
    f = pl.program_id(1)
    last = pl.num_programs(1) - 1

    @pl.when(i < nv_ref[0])
    def _():
        @pl.when(f == 0)
        def _():
            acc_ref[...] = jnp.zeros_like(acc_ref)
            xb_ref[...] = _unpack_bf16_pairs(x_ref[...])

        xb = xb_ref[...]
        g = jnp.dot(xb, _bf16(wg_ref[0]), preferred_element_type=jnp.float32)
        u = jnp.dot(xb, _bf16(wu_ref[0]), preferred_element_type=jnp.float32)
        h = (_silu(g) * u).astype(jnp.bfloat16)
        acc_ref[...] += jnp.dot(h, _bf16(wd_ref[0]), preferred_element_type=jnp.float32)

        @pl.when(f == last)
        def _():
            o_ref[...] = acc_ref[...]

    @pl.when((i >= nv_ref[0]) & (f == last))
    def _():
        o_ref[...] = jnp.zeros_like(o_ref)


def _experts(blk_expert, n_valid, x_rows, w_gate, w_up, w_down):
    n_rows, dw = x_rows.shape
    d = 2 * dw
    ff = w_gate.shape[2]
    tm, tf = MOE_ROW_TILE, FFN_COL_TILE
    n_f = ff // tf

    def fcol(i, f, nv):
        return jnp.where(i < nv[0], f, n_f - 1)

    return pl.pallas_call(
        _experts_kernel,
        grid_spec=pltpu.PrefetchScalarGridSpec(
            num_scalar_prefetch=2,
            grid=(n_rows // tm, n_f),
            in_specs=[pl.BlockSpec((tm, dw), lambda i, f, be, nv: (i, 0)),
                      pl.BlockSpec((1, d, tf), lambda i, f, be, nv: (be[i], 0, fcol(i, f, nv))),
                      pl.BlockSpec((1, d, tf), lambda i, f, be, nv: (be[i], 0, fcol(i, f, nv))),
                      pl.BlockSpec((1, tf, d), lambda i, f, be, nv: (be[i], fcol(i, f, nv), 0))],
            out_specs=pl.BlockSpec((tm, d), lambda i, f, be, nv: (i, 0)),
            scratch_shapes=[pltpu.VMEM((tm, d), jnp.float32), pltpu.VMEM((tm, d), jnp.bfloat16)]),
        out_shape=jax.ShapeDtypeStruct((n_rows, d), jnp.float32),
        compiler_params=_params(("arbitrary", "arbitrary")),
        name="moe_experts",
    )(blk_expert, n_valid, x_rows, w_gate, w_up, w_down)


def _combine_kernel(dest_ref, y_hbm, x_ref, gate_ref, o_ref, buf, sem):
    base = pl.program_id(0) * GATHER_ROWS * TOP_K

    def issue(r, carry):
        for k in range(TOP_K):
            _row_copy(y_hbm, buf.at[k], dest_ref[base + r * TOP_K + k], r, sem).start()
        return carry

    lax.fori_loop(0, GATHER_ROWS, issue, 0)
    for k in range(TOP_K):
        pltpu.make_async_copy(y_hbm.at[pl.ds(0, GATHER_ROWS)], buf.at[k], sem).wait()
    gate = gate_ref[...]
    o_ref[...] = x_ref[...] + gate[:, 0:1] * buf[0] + gate[:, 1:2] * buf[1]


def _combine(dest, y_rows, x, gates):
    n, d = x.shape
    return pl.pallas_call(
        _combine_kernel,
        grid_spec=pltpu.PrefetchScalarGridSpec(
            num_scalar_prefetch=1,
            grid=(n // GATHER_ROWS,),
            in_specs=[pl.BlockSpec(memory_space=pl.ANY),
                      pl.BlockSpec((GATHER_ROWS, d), lambda i, dest: (i, 0)),
                      pl.BlockSpec((GATHER_ROWS, LANES), lambda i, dest: (i, 0))],
            out_specs=pl.BlockSpec((GATHER_ROWS, d), lambda i, dest: (i, 0)),
            scratch_shapes=[pltpu.VMEM((TOP_K, GATHER_ROWS, d), jnp.float32),
                            pltpu.SemaphoreType.DMA(())]),
        out_shape=jax.ShapeDtypeStruct((n, d), jnp.float32),
        compiler_params=_params(("arbitrary",)),
        name="moe_combine",
    )(dest, y_rows, x, gates)


def _moe(x, g, w_router, w_gate, w_up, w_down, layer):
    n = x.shape[0]
    w_gate, w_up, w_down = (w.reshape((-1,) + w.shape[2:]) for w in (w_gate, w_up, w_down))
    xn, idx, gates = _router(x, g, w_router)
    e_flat = idx[:, :TOP_K].reshape(-1)
    onehot = (e_flat[:, None] == jnp.arange(N_EXPERTS, dtype=jnp.int32)[None, :]).astype(jnp.int32)
    counts = jnp.sum(onehot, axis=0)
    rank = jnp.sum((jnp.cumsum(onehot, axis=0) - onehot) * onehot, axis=1)
    padded = (counts + MOE_ROW_TILE - 1) // MOE_ROW_TILE * MOE_ROW_TILE
    pad_end = jnp.cumsum(padded)
    pad_start = pad_end - padded
    dest = (pad_start[e_flat] + rank).astype(jnp.int32)
    n_rows = n * TOP_K + N_EXPERTS * MOE_ROW_TILE
    tok_flat = jnp.repeat(jnp.arange(n, dtype=jnp.int32), TOP_K)
    row_tok = jnp.zeros((n_rows,), jnp.int32).at[dest].set(tok_flat)
    n_blk = n_rows // MOE_ROW_TILE
    blk_expert = jnp.clip(jnp.searchsorted(pad_end, jnp.arange(n_blk, dtype=jnp.int32) * MOE_ROW_TILE, side='right'),
                          0, N_EXPERTS - 1).astype(jnp.int32)
    n_valid = (pad_end[-1] // MOE_ROW_TILE).astype(jnp.int32).reshape(1)
    x_rows = _gather_rows(row_tok, xn)
    y_rows = _experts(blk_expert + layer * N_EXPERTS, n_valid, x_rows, w_gate, w_up, w_down)
    return _combine(dest, y_rows, x, gates)


def _pad_cols(w, width):
    return jnp.pad(w, ((0, 0), (0, width - w.shape[1])))


def _even_layer(x, batch, seq, layer, norm_mix, w_in, w_out, norm_ffn, w_gate, w_up, w_down, tabs64, tabs32):
    k_b, k_a, k_i, qb_t, vb_t, qa_t, va_t, qi_t, w_t = _even_in(x, norm_mix, w_in, tabs64, tabs32, seq=seq)
    out_a = _dsa(qa_t, k_a, va_t, qi_t, k_i, w_t, batch=batch, seq=seq)
    out_b = _dilated(qb_t, k_b, vb_t, batch=batch, seq=seq)
    x = _out_proj(out_a, out_b, w_out, x)
    return _ffn(x, norm_ffn, w_gate, w_up, w_down, layer)


def _odd_layer(x, batch, seq, layer, norm_mix, w_in, q_norm, w_uq, kv_norm, w_ukv, conv_w, conv_b, w_a, b_a, w_x, b_x, lam,
               w_out, norm_ffn, w_router, w_gate, w_up, w_down, tabs_mla):
    xg, k_rope, qn_t, qr_t, kn, v_t = _odd_in(x, norm_mix, w_in, q_norm, w_uq, kv_norm, w_ukv, tabs_mla, seq=seq)
    out_c = _mla(qn_t, qr_t, kn, k_rope, v_t, batch=batch, seq=seq)
    out_d = _rglru(xg.reshape(batch, seq, -1), conv_w, conv_b, w_a, b_a, w_x, b_x, lam,
                   batch=batch, seq=seq).reshape(batch * seq, -1)
    x = _out_proj(out_c, out_d, w_out, x)
    return _moe(x, norm_ffn, w_router, w_gate, w_up, w_down, layer)


def kernel(x, ev_norm_mix, ev_w_in, ev_w_out, ev_norm_ffn, ffn_w_gate, ffn_w_up, ffn_w_down, od_norm_mix, od_w_in, mla_q_norm, mla_w_uq, mla_kv_norm, mla_w_ukv, rg_conv_w, rg_conv_b, rg_w_a, rg_b_a, rg_w_x, rg_b_x, rg_lambda, od_w_out, od_norm_ffn, moe_router, moe_w_gate, moe_w_up, moe_w_down, final_norm):
    batch, seq, d = x.shape
    depth = ev_w_in.shape[0] + od_w_in.shape[0]
    assert d == D_MODEL and seq % KEY_CHUNK == 0 and (batch * seq) % FFN_ROW_TILE == 0
    tabs64 = _rope_tables(seq, HEAD_DIM, ROT_DIM)
    tabs32 = _rope_tables(seq, IDX_DIM, IDX_ROT)
    tabs_mla = _rope_tables(seq, C_ROPE, C_ROPE)
    h = x.reshape(batch * seq, d)
    for layer in range(depth):
        i = layer // 2
        if layer % 2 == 0:
            h = _even_layer(h, batch, seq, i, ev_norm_mix[i], ev_w_in[i], ev_w_out[i], ev_norm_ffn[i],
                            ffn_w_gate, ffn_w_up, ffn_w_down, tabs64, tabs32)
        else:
            h = _odd_layer(h, batch, seq, i, od_norm_mix[i], od_w_in[i], mla_q_norm[i], mla_w_uq[i], mla_kv_norm[i],
                           mla_w_ukv[i], rg_conv_w[i], rg_conv_b[i], rg_w_a[i], rg_b_a[i], rg_w_x[i], rg_b_x[i],
                           rg_lambda[i], od_w_out[i], od_norm_ffn[i], moe_router[i], moe_w_gate, moe_w_up,
                           moe_w_down, tabs_mla)
    out = _rmsnorm(h, final_norm, out_dtype=jnp.float32, name="rmsnorm_final")
    return out.reshape(batch, seq, d)
```
